```python
import jax, jax.numpy as jnp
from jax import lax
import numpy as np

D_MODEL = 1024
BATCH = 8
SEQ = 4096
DEPTH = 2

CHUNK = 64
N_PREV_CHUNKS = 8
BAND = (N_PREV_CHUNKS + 1) * CHUNK
HEAD_DIM = 64
H_A = 8
H_B = 8
W_A = H_A * HEAD_DIM
W_B = H_B * HEAD_DIM
W_MIX = W_A + W_B
W_IN = 3 * W_MIX
REL_CLIP = 128
SB_BLOCK = 128
D_FF = 2816
CONV_WIDTH = 3
N_MOD = 6
EPS = 1e-6

kernel_name = "hybrid_chunked_stickbreaking_convffn"


def rms_norm(x):
    xf = x.astype(jnp.float32)
    y = xf * lax.rsqrt(jnp.mean(xf * xf, axis=-1, keepdims=True) + EPS)
    return y.astype(x.dtype)


def modulate(h, shift, scale):
    return h * (1 + scale[:, None, :]) + shift[:, None, :]


def rel_bias_band(rel_bias):
    q_pos = N_PREV_CHUNKS * CHUNK + jnp.arange(CHUNK)
    k_pos = jnp.arange(BAND)
    dist = jnp.clip(q_pos[:, None] - k_pos[None, :], -REL_CLIP, REL_CLIP) + REL_CLIP
    return rel_bias[:, dist]


def chunked_rel_attention(q, k, v, rel_bias):
    b, s, h, dh = q.shape
    n_chunks = s // CHUNK
    qc = q.reshape(b, n_chunks, CHUNK, h, dh)
    pad = ((0, 0), (N_PREV_CHUNKS, 0), (0, 0), (0, 0), (0, 0))
    kp = jnp.pad(k.reshape(b, n_chunks, CHUNK, h, dh), pad)
    vp = jnp.pad(v.reshape(b, n_chunks, CHUNK, h, dh), pad)
    band_idx = jnp.arange(n_chunks)[:, None] + jnp.arange(N_PREV_CHUNKS + 1)[None, :]
    kb = kp[:, band_idx].reshape(b, n_chunks, BAND, h, dh)
    vb = vp[:, band_idx].reshape(b, n_chunks, BAND, h, dh)
    scores = jnp.einsum("bcqhd,bckhd->bhcqk", qc, kb).astype(jnp.float32) * (dh ** -0.5)
    scores = scores + rel_bias_band(rel_bias).astype(jnp.float32)[None, :, None]
    key_chunk = (jnp.arange(n_chunks)[:, None] - N_PREV_CHUNKS
                 + (jnp.arange(BAND) // CHUNK)[None, :])
    scores = jnp.where((key_chunk >= 0)[None, None, :, None, :], scores, -jnp.inf)
    probs = jax.nn.softmax(scores, axis=-1).astype(v.dtype)
    out = jnp.einsum("bhcqk,bckhd->bcqhd", probs, vb)
    return out.reshape(b, s, h * dh)


def stick_breaking_attention(q, k, v):
    b, s, h, dh = q.shape
    scale = dh ** -0.5
    outs = []
    for blk in range(s // SB_BLOCK):
        q_start = blk * SB_BLOCK
        k_end = q_start + SB_BLOCK
        logits = jnp.einsum("bqhd,bkhd->bhqk", q[:, q_start:k_end],
                            k[:, :k_end]).astype(jnp.float32) * scale
        strict = (q_start + jnp.arange(SB_BLOCK))[:, None] > jnp.arange(k_end)[None, :]
        log_beta = jax.nn.log_sigmoid(logits)
        log_keep = jnp.where(strict, jax.nn.log_sigmoid(-logits), 0.0)
        log_tail = lax.cumsum(log_keep, axis=3, reverse=True) - log_keep
        weights = jnp.where(strict, jnp.exp(log_beta + log_tail), 0.0).astype(v.dtype)
        outs.append(jnp.einsum("bhqk,bkhd->bqhd", weights, v[:, :k_end]))
    return jnp.concatenate(outs, axis=1).reshape(b, s, h * dh)


def causal_depthwise_conv(u, w, bias):
    s = u.shape[1]
    up = jnp.pad(u, ((0, 0), (CONV_WIDTH - 1, 0), (0, 0)))
    y = w[0] * up[:, 0:s]
    for i in range(1, CONV_WIDTH):
        y = y + w[i] * up[:, i:i + s]
    return y + bias


def hybrid_layer(x, c_act, w_ada, b_ada, w_in, rel_bias, g_a, g_b, w_out,
                 w_up, conv_w, conv_b, w_down):
    b, s, _ = x.shape
    mod = c_act @ w_ada + b_ada
    shift_mix, scale_mix, gate_mix, shift_ffn, scale_ffn, gate_ffn = jnp.split(mod, N_MOD, axis=-1)

    h = modulate(rms_norm(x), shift_mix, scale_mix)
    proj = h @ w_in
    cuts = [W_A, 2 * W_A, 3 * W_A, 3 * W_A + W_B, 3 * W_A + 2 * W_B]
    qa, ka, va, qb, kb, vb = jnp.split(proj, cuts, axis=-1)
    oa = chunked_rel_attention(qa.reshape(b, s, H_A, HEAD_DIM), ka.reshape(b, s, H_A, HEAD_DIM),
                               va.reshape(b, s, H_A, HEAD_DIM), rel_bias)
    ob = stick_breaking_attention(qb.reshape(b, s, H_B, HEAD_DIM), kb.reshape(b, s, H_B, HEAD_DIM),
                                  vb.reshape(b, s, H_B, HEAD_DIM))
    mixed = jnp.concatenate([rms_norm(oa) * g_a, rms_norm(ob) * g_b], axis=-1) @ w_out
    x = x + gate_mix[:, None, :] * mixed

    h = modulate(rms_norm(x), shift_ffn, scale_ffn)
    gate, val = jnp.split(causal_depthwise_conv(h @ w_up, conv_w, conv_b), 2, axis=-1)
    x = x + gate_ffn[:, None, :] * ((jax.nn.silu(gate) * val) @ w_down)
    return x


def setup_inputs(seed: int = 0) -> dict:
    key = jax.random.key(seed)
    ks = jax.random.split(key, 14)
    f32 = jnp.float32
    nrm = lambda k, shape: jax.random.normal(k, shape, dtype=f32)
    return {
        "x": nrm(ks[0], (BATCH, SEQ, D_MODEL)),
        "c": nrm(ks[1], (BATCH, D_MODEL)),
        "w_ada": nrm(ks[2], (DEPTH, D_MODEL, N_MOD * D_MODEL)) * D_MODEL ** -0.5,
        "b_ada": nrm(ks[3], (DEPTH, N_MOD * D_MODEL)) * 0.02,
        "w_in": nrm(ks[4], (DEPTH, D_MODEL, W_IN)) * D_MODEL ** -0.5,
        "rel_bias": nrm(ks[5], (DEPTH, H_A, 2 * REL_CLIP + 1)) * 0.5,
        "g_a": 1.0 + 0.1 * nrm(ks[6], (DEPTH, W_A)),
        "g_b": 1.0 + 0.1 * nrm(ks[7], (DEPTH, W_B)),
        "w_out": nrm(ks[8], (DEPTH, W_MIX, D_MODEL)) * W_MIX ** -0.5,
        "w_up": nrm(ks[9], (DEPTH, D_MODEL, 2 * D_FF)) * D_MODEL ** -0.5,
        "conv_w": nrm(ks[10], (DEPTH, CONV_WIDTH, 2 * D_FF)) * CONV_WIDTH ** -0.5,
        "conv_b": nrm(ks[11], (DEPTH, 2 * D_FF)) * 0.02,
        "w_down": nrm(ks[12], (DEPTH, D_FF, D_MODEL)) * D_FF ** -0.5,
        "final_g": 1.0 + 0.1 * nrm(ks[13], (D_MODEL,)),
    }


def reference(x, c, w_ada, b_ada, w_in, rel_bias, g_a, g_b, w_out, w_up, conv_w,
              conv_b, w_down, final_g):
    c_act = jax.nn.silu(c)
    for l in range(DEPTH):
        x = hybrid_layer(x, c_act, w_ada[l], b_ada[l], w_in[l], rel_bias[l], g_a[l], g_b[l],
                         w_out[l], w_up[l], conv_w[l], conv_b[l], w_down[l])
    return rms_norm(x) * final_g
```

```python
import functools

import jax
import jax.numpy as jnp
from jax import lax
from jax.experimental import pallas as pl
from jax.experimental.pallas import tpu as pltpu

F32 = jnp.float32
BF16 = jnp.bfloat16

D_MODEL = 1024
CHUNK = 64
N_PREV_CHUNKS = 8
HEAD_DIM = 64
H_A = 8
H_B = 8
W_A = H_A * HEAD_DIM
W_B = H_B * HEAD_DIM
W_IN = 3 * (W_A + W_B)
REL_CLIP = 128
D_FF = 2816
N_MOD = 6
EPS = 1e-6
QK_SCALE = HEAD_DIM ** -0.5

LANES = 128
HEADS_PER_BLOCK = LANES // HEAD_DIM
MASKED = -1e30

ATT_TQ = 128
ATT_WIN = N_PREV_CHUNKS * CHUNK + ATT_TQ
ATT_BIAS_TILES = (N_PREV_CHUNKS * CHUNK + ATT_WIN) // LANES

SB_T = 256
SB_DEAD = -104.0

FFN_TM = 256
FFN_CHUNK = 256
VMEM_LIMIT = 56 * 1024 * 1024


def _dot(a, b):
    return jnp.dot(a, b, preferred_element_type=F32)


def _dot_nt(a, b):
    return lax.dot_general(a, b, (((1,), (1,)), ((), ())), preferred_element_type=F32)


def _rms(x):
    return x * lax.rsqrt(jnp.mean(x * x, axis=-1, keepdims=True) + EPS)


def _split_bf16(x):
    hi = x.astype(BF16)
    lo = (x - hi.astype(F32)).astype(BF16)
    return hi, lo


def _mod_kernel(c_ref, w_ref, b_ref, o_ref):
    c = c_ref[...]
    ca = c / (1.0 + jnp.exp(-c))
    a_hi, a_lo = _split_bf16(ca)
    w_hi, w_lo = _split_bf16(w_ref[0])
    acc = _dot(a_hi, w_hi) + _dot(a_hi, w_lo) + _dot(a_lo, w_hi)
    o_ref[0] = acc + b_ref[0]


def _modulation(c, w_ada, b_ada):
    depth, d, n = w_ada.shape
    b = c.shape[0]
    tn = 1536
    return pl.pallas_call(
        _mod_kernel,
        grid=(depth, n // tn),
        in_specs=[
            pl.BlockSpec((b, d), lambda l, j: (0, 0)),
            pl.BlockSpec((1, d, tn), lambda l, j: (l, 0, j)),
            pl.BlockSpec((1, 1, tn), lambda l, j: (l, 0, j)),
        ],
        out_specs=pl.BlockSpec((1, b, tn), lambda l, j: (l, 0, j)),
        out_shape=jax.ShapeDtypeStruct((depth, b, n), F32),
        compiler_params=pltpu.CompilerParams(
            dimension_semantics=("arbitrary", "arbitrary"), vmem_limit_bytes=VMEM_LIMIT),
        name="adaln_mod",
    )(c, w_ada, b_ada.reshape(depth, 1, n))


def _bias_kernel(rb_ref, o_ref):
    lh = pl.program_id(0)
    i = lax.broadcasted_iota(jnp.int32, (ATT_TQ, LANES), 0)
    past = N_PREV_CHUNKS * CHUNK
    for ct in range(ATT_BIAS_TILES):
        if (ct + 1) * LANES > ATT_WIN:
            o_ref[0, ct] = jnp.full((ATT_TQ, LANES), MASKED, F32)
            continue
        r = lax.broadcasted_iota(jnp.int32, (ATT_TQ, LANES), 1) + ct * LANES
        dist = jnp.clip(past + i - r, -REL_CLIP, REL_CLIP) + REL_CLIP
        d_lo = min(max(past - (ct + 1) * LANES + 1, -REL_CLIP), REL_CLIP) + REL_CLIP
        d_hi = min(max(past + ATT_TQ - 1 - ct * LANES, -REL_CLIP), REL_CLIP) + REL_CLIP

        def body(d, acc, dist=dist):
            return jnp.where(dist == d, rb_ref[lh, d], acc)

        acc = lax.fori_loop(d_lo, d_hi + 1, body, jnp.zeros((ATT_TQ, LANES), F32))
        qc = i >> 6
        kc = r >> 6
        visible = (kc >= qc) & (kc <= qc + N_PREV_CHUNKS)
        o_ref[0, ct] = jnp.where(visible, acc, MASKED)


def _bias_tables(rel_bias):
    depth, h, nrel = rel_bias.shape
    return pl.pallas_call(
        _bias_kernel,
        grid=(depth * h,),
        in_specs=[pl.BlockSpec(memory_space=pltpu.SMEM)],
        out_specs=pl.BlockSpec((1, ATT_BIAS_TILES, ATT_TQ, LANES), lambda n: (n, 0, 0, 0)),
        out_shape=jax.ShapeDtypeStruct((depth * h, ATT_BIAS_TILES, ATT_TQ, LANES), F32),
        compiler_params=pltpu.CompilerParams(dimension_semantics=("arbitrary",)),
        name="rel_bias_table",
    )(rel_bias.reshape(depth * h, nrel))


def _in_proj_kernel(x_ref, mod_ref, w_ref, o_ref):
    x = x_ref[0]
    shift = mod_ref[0, 0:1, :]
    scale = mod_ref[0, 1:2, :]
    h = _rms(x) * (1.0 + scale) + shift
    o_ref[0] = _dot(h.astype(BF16), w_ref[...]).astype(BF16)


def _in_proj(x, mod, w_in):
    b, s, d = x.shape
    n = w_in.shape[1]
    tm = 512
    return pl.pallas_call(
        _in_proj_kernel,
        grid=(b, s // tm),
        in_specs=[
            pl.BlockSpec((1, tm, d), lambda i, j: (i, j, 0)),
            pl.BlockSpec((1, N_MOD, d), lambda i, j: (i, 0, 0)),
            pl.BlockSpec((d, n), lambda i, j: (0, 0)),
        ],
        out_specs=pl.BlockSpec((1, tm, n), lambda i, j: (i, j, 0)),
        out_shape=jax.ShapeDtypeStruct((b, s, n), BF16),
        compiler_params=pltpu.CompilerParams(
            dimension_semantics=("arbitrary", "arbitrary"), vmem_limit_bytes=VMEM_LIMIT),
        name="in_proj",
    )(x, mod, w_in)


def _attn_a_kernel(q_ref, k_ref, v_ref, bias_ref, o_ref):
    q0 = pl.program_id(2) * ATT_TQ
    past = N_PREV_CHUNKS * CHUNK
    start = pl.multiple_of(jnp.maximum(q0 - past, 0), LANES)
    bias_tile = jnp.maximum(past - q0, 0) // LANES
    q = q_ref[0] * QK_SCALE
    kw = k_ref[0, pl.ds(start, ATT_WIN), :]
    vw = v_ref[0, pl.ds(start, ATT_WIN), :]
    lane = lax.broadcasted_iota(jnp.int32, (ATT_TQ, LANES), 1)
    outs = []
    for h in range(HEADS_PER_BLOCK):
        in_head = (lane >= h * HEAD_DIM) & (lane < (h + 1) * HEAD_DIM)
        qh = jnp.where(in_head, q, jnp.zeros_like(q))
        bias = jnp.concatenate(
            [bias_ref[h, bias_tile + t] for t in range(ATT_WIN // LANES)], axis=1)
        s = _dot_nt(qh, kw) + bias
        m = jnp.max(s, axis=-1, keepdims=True)
        p = jnp.exp(s - m)
        denom = jnp.sum(p, axis=-1, keepdims=True)
        outs.append(_dot(p.astype(BF16), vw) / denom)
    o_ref[0] = jnp.where(lane < HEAD_DIM, outs[0], outs[1]).astype(BF16)


def _attn_a(proj, bias):
    b, s, _ = proj.shape
    blocks = W_A // LANES
    return pl.pallas_call(
        _attn_a_kernel,
        grid=(b, blocks, s // ATT_TQ),
        in_specs=[
            pl.BlockSpec((1, ATT_TQ, LANES), lambda i, hp, j: (i, j, hp)),
            pl.BlockSpec((1, s, LANES), lambda i, hp, j: (i, 0, blocks + hp)),
            pl.BlockSpec((1, s, LANES), lambda i, hp, j: (i, 0, 2 * blocks + hp)),
            pl.BlockSpec((HEADS_PER_BLOCK, ATT_BIAS_TILES, ATT_TQ, LANES),
                         lambda i, hp, j: (hp, 0, 0, 0)),
        ],
        out_specs=pl.BlockSpec((1, ATT_TQ, LANES), lambda i, hp, j: (i, j, hp)),
        out_shape=jax.ShapeDtypeStruct((b, s, W_A), BF16),
        compiler_params=pltpu.CompilerParams(
            dimension_semantics=("arbitrary", "arbitrary", "arbitrary"),
            vmem_limit_bytes=VMEM_LIMIT),
        name="chunked_attn",
    )(proj, proj, proj, bias)


def _sb_kernel(q_ref, k_ref, v_ref, o_ref, acc_ref, carry_ref):
    qi = pl.program_id(2)
    q = q_ref[0] * QK_SCALE
    lane = lax.broadcasted_iota(jnp.int32, (SB_T, LANES), 1)
    row = lax.broadcasted_iota(jnp.int32, (SB_T, SB_T), 0)
    col = lax.broadcasted_iota(jnp.int32, (SB_T, SB_T), 1)
    strict = row > col
    later = jnp.where(strict, 1.0, 0.0).astype(BF16)
    qm = []
    for h in range(HEADS_PER_BLOCK):
        in_head = (lane >= h * HEAD_DIM) & (lane < (h + 1) * HEAD_DIM)
        qm.append(jnp.where(in_head, q, jnp.zeros_like(q)))

    def tile(t, diag):
        ks = pl.multiple_of(t * SB_T, SB_T)
        kt = k_ref[0, pl.ds(ks, SB_T), :]
        vt = v_ref[0, pl.ds(ks, SB_T), :]
        for h in range(HEADS_PER_BLOCK):
            s = _dot_nt(qm[h], kt)
            softplus_neg_abs = jnp.log(1.0 + jnp.exp(-jnp.abs(s)))
            log_beta = jnp.minimum(s, 0.0) - softplus_neg_abs
            log_keep = log_beta - s
            if diag:
                log_keep = jnp.where(strict, log_keep, 0.0)
            hi, lo = _split_bf16(log_keep)
            log_tail = _dot(hi, later) + _dot(lo, later)
            if not diag:
                log_tail = log_tail + carry_ref[h]
            w = jnp.exp(log_beta + log_tail)
            if diag:
                w = jnp.where(strict, w, 0.0)
            pv = _dot(w.astype(BF16), vt)
            keep_sum = jnp.sum(log_keep, axis=-1, keepdims=True)
            if diag:
                acc_ref[h] = pv
                carry_ref[h] = keep_sum
            else:
                acc_ref[h] += pv
                carry_ref[h] += keep_sum

    tile(qi, True)

    def cond(t):
        return jnp.logical_and(t >= 0, jnp.max(carry_ref[...]) > SB_DEAD)

    def body(t):
        tile(t, False)
        return t - 1

    lax.while_loop(cond, body, qi - 1)
    o_ref[0] = jnp.where(lane < HEAD_DIM, acc_ref[0], acc_ref[1]).astype(BF16)


def _attn_sb(proj):
    b, s, _ = proj.shape
    blocks = W_B // LANES
    base = 3 * W_A // LANES
    return pl.pallas_call(
        _sb_kernel,
        grid=(b, blocks, s // SB_T),
        in_specs=[
            pl.BlockSpec((1, SB_T, LANES), lambda i, hp, j: (i, j, base + hp)),
            pl.BlockSpec((1, s, LANES), lambda i, hp, j: (i, 0, base + blocks + hp)),
            pl.BlockSpec((1, s, LANES), lambda i, hp, j: (i, 0, base + 2 * blocks + hp)),
        ],
        out_specs=pl.BlockSpec((1, SB_T, LANES), lambda i, hp, j: (i, j, hp)),
        out_shape=jax.ShapeDtypeStruct((b, s, W_B), BF16),
        scratch_shapes=[
            pltpu.VMEM((HEADS_PER_BLOCK, SB_T, LANES), F32),
            pltpu.VMEM((HEADS_PER_BLOCK, SB_T, 1), F32),
        ],
        compiler_params=pltpu.CompilerParams(
            dimension_semantics=("arbitrary", "arbitrary", "arbitrary"),
            vmem_limit_bytes=VMEM_LIMIT),
        name="stick_breaking_attn",
    )(proj, proj, proj)


def _ffn_kernel(x_ref, oa_ref, ob_ref, mod_ref, ga_ref, gb_ref, wout_ref, wup_ref,
                cw_ref, cb_ref, wdown_ref, fg_ref, o_ref, act_ref, tail_ref, *, final):
    tm = FFN_TM

    @pl.when(pl.program_id(1) == 0)
    def _():
        tail_ref[...] = jnp.zeros_like(tail_ref)

    gate_mix = mod_ref[0, 2:3, :]
    shift = mod_ref[0, 3:4, :]
    scale = mod_ref[0, 4:5, :]
    gate_ffn = mod_ref[0, 5:6, :]

    na = _rms(oa_ref[0].astype(F32)) * ga_ref[...]
    nb = _rms(ob_ref[0].astype(F32)) * gb_ref[...]
    mix_in = jnp.concatenate([na, nb], axis=-1).astype(BF16)
    x1 = x_ref[0] + gate_mix * _dot(mix_in, wout_ref[...])

    h = (_rms(x1) * (1.0 + scale) + shift).astype(BF16)
    row = lax.broadcasted_iota(jnp.int32, (tm, FFN_CHUNK), 0)

    def conv(cols):
        up = _dot(h, wup_ref[:, cols])
        prev = tail_ref[:, cols]
        tail_ref[:, cols] = up[tm - 8:, :]
        m1 = jnp.where(row == 0, prev[7:8, :], pltpu.roll(up, 1, axis=0))
        m2 = jnp.where(row == 0, prev[6:7, :],
                       jnp.where(row == 1, prev[7:8, :], pltpu.roll(up, 2, axis=0)))
        return (cw_ref[0:1, cols] * m2 + cw_ref[1:2, cols] * m1
                + cw_ref[2:3, cols] * up + cb_ref[:, cols])

    for j in range(D_FF // FFN_CHUNK):
        g = conv(slice(j * FFN_CHUNK, (j + 1) * FFN_CHUNK))
        v = conv(slice(D_FF + j * FFN_CHUNK, D_FF + (j + 1) * FFN_CHUNK))
        act = g / (1.0 + jnp.exp(-g)) * v
        act_ref[:, j * FFN_CHUNK:(j + 1) * FFN_CHUNK] = act.astype(BF16)

    x2 = x1 + gate_ffn * _dot(act_ref[...], wdown_ref[...])
    if final:
        x2 = _rms(x2) * fg_ref[...]
    o_ref[0] = x2


def _ffn(x, oa, ob, mod, g_a, g_b, w_out, w_up, conv_w, conv_b, w_down, final_g, final):
    b, s, d = x.shape
    tm = FFN_TM
    const = lambda i, j: (0, 0)
    tok = lambda i, j: (i, j, 0)
    return pl.pallas_call(
        functools.partial(_ffn_kernel, final=final),
        grid=(b, s // tm),
        in_specs=[
            pl.BlockSpec((1, tm, d), tok),
            pl.BlockSpec((1, tm, W_A), tok),
            pl.BlockSpec((1, tm, W_B), tok),
            pl.BlockSpec((1, N_MOD, d), lambda i, j: (i, 0, 0)),
            pl.BlockSpec((1, W_A), const),
            pl.BlockSpec((1, W_B), const),
            pl.BlockSpec((W_A + W_B, d), const),
            pl.BlockSpec((d, 2 * D_FF), const),
            pl.BlockSpec((3, 2 * D_FF), const),
            pl.BlockSpec((1, 2 * D_FF), const),
            pl.BlockSpec((D_FF, d), const),
            pl.BlockSpec((1, d), const),
        ],
        out_specs=pl.BlockSpec((1, tm, d), tok),
        out_shape=jax.ShapeDtypeStruct((b, s, d), F32),
        scratch_shapes=[
            pltpu.VMEM((tm, D_FF), BF16),
            pltpu.VMEM((8, 2 * D_FF), F32),
        ],
        compiler_params=pltpu.CompilerParams(
            dimension_semantics=("arbitrary", "arbitrary"), vmem_limit_bytes=VMEM_LIMIT),
        name="out_proj_ffn",
    )(x, oa, ob, mod, g_a, g_b, w_out, w_up, conv_w, conv_b, w_down, final_g)


def kernel(x, c, w_ada, b_ada, w_in, rel_bias, g_a, g_b, w_out, w_up, conv_w, conv_b,
           w_down, final_g):
    depth = w_ada.shape[0]
    b = x.shape[0]
    mod = _modulation(c, w_ada, b_ada).reshape(depth, b, N_MOD, D_MODEL)
    bias = _bias_tables(rel_bias).reshape(depth, H_A, ATT_BIAS_TILES, ATT_TQ, LANES)
    for l in range(depth):
        proj = _in_proj(x, mod[l], w_in[l].astype(BF16))
        oa = _attn_a(proj, bias[l])
        ob = _attn_sb(proj)
        x = _ffn(x, oa, ob, mod[l], g_a[l][None], g_b[l][None], w_out[l].astype(BF16),
                 w_up[l].astype(BF16), conv_w[l], conv_b[l][None], w_down[l].astype(BF16),
                 final_g[None], final=(l == depth - 1))
    return x
```

```python
import functools

import jax
import jax.numpy as jnp
from jax import lax
from jax.experimental import pallas as pl
from jax.experimental.pallas import tpu as pltpu

F32 = jnp.float32
BF16 = jnp.bfloat16

D_MODEL = 1024
CHUNK = 64
N_PREV_CHUNKS = 8
HEAD_DIM = 64
H_A = 8
H_B = 8
W_A = H_A * HEAD_DIM
W_B = H_B * HEAD_DIM
W_IN = 3 * (W_A + W_B)
REL_CLIP = 128
D_FF = 2816
N_MOD = 6
EPS = 1e-6
QK_SCALE = HEAD_DIM ** -0.5

LANES = 128
HEADS_PER_BLOCK = LANES // HEAD_DIM
MASKED = -1e30

ATT_TQ = 128
ATT_GROUPS = 4
ATT_WIN = N_PREV_CHUNKS * CHUNK + ATT_TQ
ATT_BIAS_TILES = (N_PREV_CHUNKS * CHUNK + ATT_WIN) // LANES

SB_T = 256
SB_R = 128
SB_GROUPS = 2
SB_DEAD = -104.0

FFN_TM = 256
FFN_CHUNK = 256
VMEM_LIMIT = 56 * 1024 * 1024


def _dot(a, b):
    return jnp.dot(a, b, preferred_element_type=F32)


def _dot_nt(a, b):
    return lax.dot_general(a, b, (((1,), (1,)), ((), ())), preferred_element_type=F32)


def _rms(x):
    return x * lax.rsqrt(jnp.mean(x * x, axis=-1, keepdims=True) + EPS)


def _neg_abs(x):
    bits = lax.bitcast_convert_type(x, jnp.int32) | jnp.int32(-2 ** 31)
    return lax.bitcast_convert_type(bits, F32)


def _split_bf16(x):
    hi = x.astype(BF16)
    lo = (x - hi.astype(F32)).astype(BF16)
    return hi, lo


def _mod_kernel(c_ref, w_ref, b_ref, o_ref):
    c = c_ref[...]
    ca = c / (1.0 + jnp.exp(-c))
    a_hi, a_lo = _split_bf16(ca)
    w_hi, w_lo = _split_bf16(w_ref[0])
    acc = _dot(a_hi, w_hi) + _dot(a_hi, w_lo) + _dot(a_lo, w_hi)
    o_ref[0] = acc + b_ref[0]


def _modulation(c, w_ada, b_ada):
    depth, d, n = w_ada.shape
    b = c.shape[0]
    tn = 1536
    return pl.pallas_call(
        _mod_kernel,
        grid=(depth, n // tn),
        in_specs=[
            pl.BlockSpec((b, d), lambda l, j: (0, 0)),
            pl.BlockSpec((1, d, tn), lambda l, j: (l, 0, j)),
            pl.BlockSpec((1, 1, tn), lambda l, j: (l, 0, j)),
        ],
        out_specs=pl.BlockSpec((1, b, tn), lambda l, j: (l, 0, j)),
        out_shape=jax.ShapeDtypeStruct((depth, b, n), F32),
        compiler_params=pltpu.CompilerParams(
            dimension_semantics=("arbitrary", "arbitrary"), vmem_limit_bytes=VMEM_LIMIT),
        name="adaln_mod",
    )(c, w_ada, b_ada.reshape(depth, 1, n))


def _bias_kernel(rb_ref, o_ref):
    lh = pl.program_id(0)
    i = lax.broadcasted_iota(jnp.int32, (ATT_TQ, LANES), 0)
    past = N_PREV_CHUNKS * CHUNK
    for ct in range(ATT_BIAS_TILES):
        if (ct + 1) * LANES > ATT_WIN:
            o_ref[0, ct] = jnp.full((ATT_TQ, LANES), MASKED, F32)
            continue
        r = lax.broadcasted_iota(jnp.int32, (ATT_TQ, LANES), 1) + ct * LANES
        dist = jnp.clip(past + i - r, -REL_CLIP, REL_CLIP) + REL_CLIP
        d_lo = min(max(past - (ct + 1) * LANES + 1, -REL_CLIP), REL_CLIP) + REL_CLIP
        d_hi = min(max(past + ATT_TQ - 1 - ct * LANES, -REL_CLIP), REL_CLIP) + REL_CLIP

        def body(d, acc, dist=dist):
            return jnp.where(dist == d, rb_ref[lh, d], acc)

        acc = lax.fori_loop(d_lo, d_hi + 1, body, jnp.zeros((ATT_TQ, LANES), F32))
        qc = i >> 6
        kc = r >> 6
        visible = (kc >= qc) & (kc <= qc + N_PREV_CHUNKS)
        o_ref[0, ct] = jnp.where(visible, acc, MASKED)


def _bias_tables(rel_bias):
    depth, h, nrel = rel_bias.shape
    return pl.pallas_call(
        _bias_kernel,
        grid=(depth * h,),
        in_specs=[pl.BlockSpec(memory_space=pltpu.SMEM)],
        out_specs=pl.BlockSpec((1, ATT_BIAS_TILES, ATT_TQ, LANES), lambda n: (n, 0, 0, 0)),
        out_shape=jax.ShapeDtypeStruct((depth * h, ATT_BIAS_TILES, ATT_TQ, LANES), F32),
        compiler_params=pltpu.CompilerParams(dimension_semantics=("arbitrary",)),
        name="rel_bias_table",
    )(rel_bias.reshape(depth * h, nrel))


def _in_proj_kernel(x_ref, mod_ref, w_ref, o_ref):
    x = x_ref[0]
    shift = mod_ref[0, 0:1, :]
    scale = mod_ref[0, 1:2, :]
    h = _rms(x) * (1.0 + scale) + shift
    o_ref[0] = _dot(h.astype(BF16), w_ref[...]).astype(BF16)


def _in_proj(x, mod, w_in):
    b, s, d = x.shape
    n = w_in.shape[1]
    tm = 512
    return pl.pallas_call(
        _in_proj_kernel,
        grid=(b, s // tm),
        in_specs=[
            pl.BlockSpec((1, tm, d), lambda i, j: (i, j, 0)),
            pl.BlockSpec((1, N_MOD, d), lambda i, j: (i, 0, 0)),
            pl.BlockSpec((d, n), lambda i, j: (0, 0)),
        ],
        out_specs=pl.BlockSpec((1, tm, n), lambda i, j: (i, j, 0)),
        out_shape=jax.ShapeDtypeStruct((b, s, n), BF16),
        compiler_params=pltpu.CompilerParams(
            dimension_semantics=("arbitrary", "arbitrary"), vmem_limit_bytes=VMEM_LIMIT),
        name="in_proj",
    )(x, mod, w_in)


def _attn_a_kernel(q_ref, k_ref, v_ref, bias_ref, o_ref, s_ref):
    q0 = pl.program_id(2) * ATT_TQ
    past = N_PREV_CHUNKS * CHUNK
    start = pl.multiple_of(jnp.maximum(q0 - past, 0), LANES)
    bias_tile = jnp.maximum(past - q0, 0) // LANES
    lane = lax.broadcasted_iota(jnp.int32, (ATT_TQ, LANES), 1)
    chains = [(g, h) for g in range(ATT_GROUPS) for h in range(HEADS_PER_BLOCK)]
    vw = [v_ref[0, pl.ds(start, ATT_WIN), g * LANES:(g + 1) * LANES] for g in range(ATT_GROUPS)]
    def qk(g, h):
        qg = q_ref[0, :, g * LANES:(g + 1) * LANES] * QK_SCALE
        kw = k_ref[0, pl.ds(start, ATT_WIN), g * LANES:(g + 1) * LANES]
        in_head = (lane >= h * HEAD_DIM) & (lane < (h + 1) * HEAD_DIM)
        return _dot_nt(jnp.where(in_head, qg, jnp.zeros_like(qg)), kw)

    outs = []
    for i, (g, h) in enumerate(chains):
        hd = g * HEADS_PER_BLOCK + h
        bias = jnp.concatenate(
            [bias_ref[hd, bias_tile + t] for t in range(ATT_WIN // LANES)], axis=1)
        s_ref[i] = qk(g, h) + bias
    for i, (g, h) in enumerate(chains):
        m = jnp.max(s_ref[i], axis=-1, keepdims=True)
        p = jnp.exp(s_ref[i] - m)
        denom = jnp.sum(p, axis=-1, keepdims=True)
        outs.append(_dot(p.astype(BF16), vw[g]) / denom)
    for g in range(ATT_GROUPS):
        o_ref[0, :, g * LANES:(g + 1) * LANES] = jnp.where(
            lane < HEAD_DIM, outs[g * HEADS_PER_BLOCK], outs[g * HEADS_PER_BLOCK + 1]
        ).astype(BF16)


def _attn_a(proj, bias):
    b, s, _ = proj.shape
    width = ATT_GROUPS * LANES
    blocks = W_A // width
    heads = ATT_GROUPS * HEADS_PER_BLOCK
    return pl.pallas_call(
        _attn_a_kernel,
        grid=(b, blocks, s // ATT_TQ),
        in_specs=[
            pl.BlockSpec((1, ATT_TQ, width), lambda i, hp, j: (i, j, hp)),
            pl.BlockSpec((1, s, width), lambda i, hp, j: (i, 0, blocks + hp)),
            pl.BlockSpec((1, s, width), lambda i, hp, j: (i, 0, 2 * blocks + hp)),
            pl.BlockSpec((heads, ATT_BIAS_TILES, ATT_TQ, LANES),
                         lambda i, hp, j: (hp, 0, 0, 0)),
        ],
        out_specs=pl.BlockSpec((1, ATT_TQ, width), lambda i, hp, j: (i, j, hp)),
        out_shape=jax.ShapeDtypeStruct((b, s, W_A), BF16),
        scratch_shapes=[pltpu.VMEM((heads, ATT_TQ, ATT_WIN), F32)],
        compiler_params=pltpu.CompilerParams(
            dimension_semantics=("arbitrary", "arbitrary", "arbitrary"),
            vmem_limit_bytes=VMEM_LIMIT),
        name="chunked_attn",
    )(proj, proj, proj, bias)


def _sb_kernel(q_ref, k_ref, v_ref, o_ref, acc_ref, carry_ref, s_ref):
    qi = pl.program_id(2)
    lane = lax.broadcasted_iota(jnp.int32, (SB_T, LANES), 1)
    row = lax.broadcasted_iota(jnp.int32, (SB_R, SB_T), 0)
    col = lax.broadcasted_iota(jnp.int32, (SB_R, SB_T), 1)
    sq_row = lax.broadcasted_iota(jnp.int32, (SB_T, SB_T), 0)
    sq_col = lax.broadcasted_iota(jnp.int32, (SB_T, SB_T), 1)
    later = jnp.where(sq_row > sq_col, 1.0, 0.0).astype(BF16)
    later2 = jnp.concatenate([later, later], axis=0)

    chains = [(g, h, r) for g in range(SB_GROUPS) for h in range(HEADS_PER_BLOCK)
              for r in range(SB_T // SB_R)]
    qm = {}
    for g in range(SB_GROUPS):
        qg = q_ref[0, :, g * LANES:(g + 1) * LANES] * QK_SCALE
        for h in range(HEADS_PER_BLOCK):
            in_head = (lane >= h * HEAD_DIM) & (lane < (h + 1) * HEAD_DIM)
            qm[g, h] = jnp.where(in_head, qg, jnp.zeros_like(qg))

    def tile(t, diag):
        ks = pl.multiple_of(t * SB_T, SB_T)
        kt = [k_ref[0, pl.ds(ks, SB_T), g * LANES:(g + 1) * LANES] for g in range(SB_GROUPS)]
        vt = [v_ref[0, pl.ds(ks, SB_T), g * LANES:(g + 1) * LANES] for g in range(SB_GROUPS)]
        for i, (g, h, r) in enumerate(chains):
            s_ref[i] = _dot_nt(qm[g, h][r * SB_R:(r + 1) * SB_R], kt[g])
        carry_scales = []
        slowest = None
        for i, (g, h, r) in enumerate(chains):
            rows = slice(r * SB_R, (r + 1) * SB_R)
            hd = g * HEADS_PER_BLOCK + h
            s = s_ref[i]
            log_beta = jnp.minimum(s, 0.0) - jnp.log(1.0 + jnp.exp(_neg_abs(s)))
            log_keep = log_beta - s
            if diag:
                log_keep = jnp.where(row + r * SB_R > col, log_keep, 0.0)
            hi, lo = _split_bf16(log_keep)
            s_ref[i] = log_beta + _dot(jnp.concatenate([hi, lo], axis=1), later2)
            carry = jnp.sum(log_keep, axis=-1, keepdims=True)
            if not diag:
                before = carry_ref[hd, rows, :]
                carry_scales.append(jnp.exp(before))
                carry = carry + before
            carry_ref[hd, rows, :] = carry
            slowest = carry if slowest is None else jnp.maximum(slowest, carry)
        for i, (g, h, r) in enumerate(chains):
            rows = slice(r * SB_R, (r + 1) * SB_R)
            hd = g * HEADS_PER_BLOCK + h
            w = jnp.exp(s_ref[i])
            if diag:
                w = jnp.where(row + r * SB_R > col, w, 0.0)
            pv = _dot(w.astype(BF16), vt[g])
            if diag:
                acc_ref[hd, rows, :] = pv
            else:
                acc_ref[hd, rows, :] += carry_scales[i] * pv
        return jnp.max(slowest) > SB_DEAD

    alive = tile(qi, True)

    def cond(state):
        t, alive = state
        return jnp.logical_and(t >= 0, alive)

    def body(state):
        t, _ = state
        return t - 1, tile(t, False)

    lax.while_loop(cond, body, (qi - 1, alive))
    for g in range(SB_GROUPS):
        o_ref[0, :, g * LANES:(g + 1) * LANES] = jnp.where(
            lane < HEAD_DIM, acc_ref[g * HEADS_PER_BLOCK], acc_ref[g * HEADS_PER_BLOCK + 1]
        ).astype(BF16)


def _attn_sb(proj):
    b, s, _ = proj.shape
    width = SB_GROUPS * LANES
    blocks = W_B // width
    base = 3 * W_A // width
    heads = SB_GROUPS * HEADS_PER_BLOCK
    return pl.pallas_call(
        _sb_kernel,
        grid=(b, blocks, s // SB_T),
        in_specs=[
            pl.BlockSpec((1, SB_T, width), lambda i, hp, j: (i, j, base + hp)),
            pl.BlockSpec((1, s, width), lambda i, hp, j: (i, 0, base + blocks + hp)),
            pl.BlockSpec((1, s, width), lambda i, hp, j: (i, 0, base + 2 * blocks + hp)),
        ],
        out_specs=pl.BlockSpec((1, SB_T, width), lambda i, hp, j: (i, j, hp)),
        out_shape=jax.ShapeDtypeStruct((b, s, W_B), BF16),
        scratch_shapes=[
            pltpu.VMEM((heads, SB_T, LANES), F32),
            pltpu.VMEM((heads, SB_T, 1), F32),
            pltpu.VMEM((heads * (SB_T // SB_R), SB_R, SB_T), F32),
        ],
        compiler_params=pltpu.CompilerParams(
            dimension_semantics=("arbitrary", "arbitrary", "arbitrary"),
            vmem_limit_bytes=VMEM_LIMIT),
        name="stick_breaking_attn",
    )(proj, proj, proj)


def _ffn_kernel(x_ref, oa_ref, ob_ref, mod_ref, ga_ref, gb_ref, wout_ref, wup_ref,
                cw_ref, cb_ref, wdown_ref, fg_ref, o_ref, act_ref, tail_ref, *, final):
    tm = FFN_TM

    @pl.when(pl.program_id(1) == 0)
    def _():
        tail_ref[...] = jnp.zeros_like(tail_ref)

    gate_mix = mod_ref[0, 2:3, :]
    shift = mod_ref[0, 3:4, :]
    scale = mod_ref[0, 4:5, :]
    gate_ffn = mod_ref[0, 5:6, :]

    na = _rms(oa_ref[0].astype(F32)) * ga_ref[...]
    nb = _rms(ob_ref[0].astype(F32)) * gb_ref[...]
    mix_in = jnp.concatenate([na, nb], axis=-1).astype(BF16)
    x1 = x_ref[0] + gate_mix * _dot(mix_in, wout_ref[...])

    h = (_rms(x1) * (1.0 + scale) + shift).astype(BF16)
    row = lax.broadcasted_iota(jnp.int32, (tm, FFN_CHUNK), 0)

    def conv(cols):
        up = _dot(h, wup_ref[:, cols])
        prev = tail_ref[:, cols]
        tail_ref[:, cols] = up[tm - 8:, :]
        m1 = jnp.where(row == 0, prev[7:8, :], pltpu.roll(up, 1, axis=0))
        m2 = jnp.where(row == 0, prev[6:7, :],
                       jnp.where(row == 1, prev[7:8, :], pltpu.roll(up, 2, axis=0)))
        return (cw_ref[0:1, cols] * m2 + cw_ref[1:2, cols] * m1
                + cw_ref[2:3, cols] * up + cb_ref[:, cols])

    for j in range(D_FF // FFN_CHUNK):
        g = conv(slice(j * FFN_CHUNK, (j + 1) * FFN_CHUNK))
        v = conv(slice(D_FF + j * FFN_CHUNK, D_FF + (j + 1) * FFN_CHUNK))
        act = g / (1.0 + jnp.exp(-g)) * v
        act_ref[:, j * FFN_CHUNK:(j + 1) * FFN_CHUNK] = act.astype(BF16)

    x2 = x1 + gate_ffn * _dot(act_ref[...], wdown_ref[...])
    if final:
        x2 = _rms(x2) * fg_ref[...]
    o_ref[0] = x2


def _ffn(x, oa, ob, mod, g_a, g_b, w_out, w_up, conv_w, conv_b, w_down, final_g, final):
    b, s, d = x.shape
    tm = FFN_TM
    const = lambda i, j: (0, 0)
    tok = lambda i, j: (i, j, 0)
    return pl.pallas_call(
        functools.partial(_ffn_kernel, final=final),
        grid=(b, s // tm),
        in_specs=[
            pl.BlockSpec((1, tm, d), tok),
            pl.BlockSpec((1, tm, W_A), tok),
            pl.BlockSpec((1, tm, W_B), tok),
            pl.BlockSpec((1, N_MOD, d), lambda i, j: (i, 0, 0)),
            pl.BlockSpec((1, W_A), const),
            pl.BlockSpec((1, W_B), const),
            pl.BlockSpec((W_A + W_B, d), const),
            pl.BlockSpec((d, 2 * D_FF), const),
            pl.BlockSpec((3, 2 * D_FF), const),
            pl.BlockSpec((1, 2 * D_FF), const),
            pl.BlockSpec((D_FF, d), const),
            pl.BlockSpec((1, d), const),
        ],
        out_specs=pl.BlockSpec((1, tm, d), tok),
        out_shape=jax.ShapeDtypeStruct((b, s, d), F32),
        scratch_shapes=[
            pltpu.VMEM((tm, D_FF), BF16),
            pltpu.VMEM((8, 2 * D_FF), F32),
        ],
        compiler_params=pltpu.CompilerParams(
            dimension_semantics=("arbitrary", "arbitrary"), vmem_limit_bytes=VMEM_LIMIT),
        name="out_proj_ffn",
    )(x, oa, ob, mod, g_a, g_b, w_out, w_up, conv_w, conv_b, w_down, final_g)


def kernel(x, c, w_ada, b_ada, w_in, rel_bias, g_a, g_b, w_out, w_up, conv_w, conv_b,
           w_down, final_g):
    depth = w_ada.shape[0]
    b = x.shape[0]
    mod = _modulation(c, w_ada, b_ada).reshape(depth, b, N_MOD, D_MODEL)
    bias = _bias_tables(rel_bias).reshape(depth, H_A, ATT_BIAS_TILES, ATT_TQ, LANES)
    for l in range(depth):
        proj = _in_proj(x, mod[l], w_in[l].astype(BF16))
        oa = _attn_a(proj, bias[l])
        ob = _attn_sb(proj)
        x = _ffn(x, oa, ob, mod[l], g_a[l][None], g_b[l][None], w_out[l].astype(BF16),
                 w_up[l].astype(BF16), conv_w[l], conv_b[l][None], w_down[l].astype(BF16),
                 final_g[None], final=(l == depth - 1))
    return x
```

```python
import functools

import jax
import jax.numpy as jnp
from jax import lax
from jax.experimental import pallas as pl
from jax.experimental.pallas import tpu as pltpu

F32 = jnp.float32
BF16 = jnp.bfloat16

D_MODEL = 1024
CHUNK = 64
N_PREV_CHUNKS = 8
HEAD_DIM = 64
H_A = 8
H_B = 8
W_A = H_A * HEAD_DIM
W_B = H_B * HEAD_DIM
W_IN = 3 * (W_A + W_B)
REL_CLIP = 128
D_FF = 2816
N_MOD = 6
EPS = 1e-6
QK_SCALE = HEAD_DIM ** -0.5

LANES = 128
HEADS_PER_BLOCK = LANES // HEAD_DIM
MASKED = -1e30
NEG_LOG2E = -1.4426950408889634

ATT_TQ = 128
ATT_GROUPS = 4
ATT_WIN = N_PREV_CHUNKS * CHUNK + ATT_TQ
ATT_BIAS_TILES = (N_PREV_CHUNKS * CHUNK + ATT_WIN) // LANES

SB_T = 256
SB_R = 128
SB_GROUPS = 2
SB_DEAD = -104.0

FFN_TM = 512
FFN_CHUNK = 256
VMEM_LIMIT = 56 * 1024 * 1024


def _dot(a, b):
    return jnp.dot(a, b, preferred_element_type=F32)


def _dot_nt(a, b):
    return lax.dot_general(a, b, (((1,), (1,)), ((), ())), preferred_element_type=F32)


def _rms(x):
    return x * lax.rsqrt(jnp.mean(x * x, axis=-1, keepdims=True) + EPS)


def _split_bf16(x):
    hi = x.astype(BF16)
    lo = (x - hi.astype(F32)).astype(BF16)
    return hi, lo


def _mod_kernel(c_ref, w_ref, b_ref, o_ref):
    c = c_ref[...]
    ca = c / (1.0 + jnp.exp(-c))
    a_hi, a_lo = _split_bf16(ca)
    w_hi, w_lo = _split_bf16(w_ref[0])
    acc = _dot(a_hi, w_hi) + _dot(a_hi, w_lo) + _dot(a_lo, w_hi)
    o_ref[0] = acc + b_ref[0]


def _modulation(c, w_ada, b_ada):
    depth, d, n = w_ada.shape
    b = c.shape[0]
    tn = 1536
    return pl.pallas_call(
        _mod_kernel,
        grid=(depth, n // tn),
        in_specs=[
            pl.BlockSpec((b, d), lambda l, j: (0, 0)),
            pl.BlockSpec((1, d, tn), lambda l, j: (l, 0, j)),
            pl.BlockSpec((1, 1, tn), lambda l, j: (l, 0, j)),
        ],
        out_specs=pl.BlockSpec((1, b, tn), lambda l, j: (l, 0, j)),
        out_shape=jax.ShapeDtypeStruct((depth, b, n), F32),
        compiler_params=pltpu.CompilerParams(
            dimension_semantics=("arbitrary", "arbitrary"), vmem_limit_bytes=VMEM_LIMIT),
        name="adaln_mod",
    )(c, w_ada, b_ada.reshape(depth, 1, n))


def _bias_kernel(rb_ref, o_ref):
    lh = pl.program_id(0)
    i = lax.broadcasted_iota(jnp.int32, (ATT_TQ, LANES), 0)
    past = N_PREV_CHUNKS * CHUNK
    for ct in range(ATT_BIAS_TILES):
        if (ct + 1) * LANES > ATT_WIN:
            o_ref[0, ct] = jnp.full((ATT_TQ, LANES), MASKED, F32)
            continue
        r = lax.broadcasted_iota(jnp.int32, (ATT_TQ, LANES), 1) + ct * LANES
        dist = jnp.clip(past + i - r, -REL_CLIP, REL_CLIP) + REL_CLIP
        d_lo = min(max(past - (ct + 1) * LANES + 1, -REL_CLIP), REL_CLIP) + REL_CLIP
        d_hi = min(max(past + ATT_TQ - 1 - ct * LANES, -REL_CLIP), REL_CLIP) + REL_CLIP

        def body(d, acc, dist=dist):
            return jnp.where(dist == d, rb_ref[lh, d], acc)

        acc = lax.fori_loop(d_lo, d_hi + 1, body, jnp.zeros((ATT_TQ, LANES), F32),
                            unroll=8)
        qc = i >> 6
        kc = r >> 6
        visible = (kc >= qc) & (kc <= qc + N_PREV_CHUNKS)
        o_ref[0, ct] = jnp.where(visible, acc, MASKED)


def _bias_tables(rel_bias):
    depth, h, nrel = rel_bias.shape
    return pl.pallas_call(
        _bias_kernel,
        grid=(depth * h,),
        in_specs=[pl.BlockSpec(memory_space=pltpu.SMEM)],
        out_specs=pl.BlockSpec((1, ATT_BIAS_TILES, ATT_TQ, LANES), lambda n: (n, 0, 0, 0)),
        out_shape=jax.ShapeDtypeStruct((depth * h, ATT_BIAS_TILES, ATT_TQ, LANES), F32),
        compiler_params=pltpu.CompilerParams(dimension_semantics=("arbitrary",)),
        name="rel_bias_table",
    )(rel_bias.reshape(depth * h, nrel))


def _in_proj_kernel(x_ref, mod_ref, w_ref, o_ref):
    x = x_ref[0]
    shift = mod_ref[0, 0:1, :]
    scale = mod_ref[0, 1:2, :]
    h = _rms(x) * (1.0 + scale) + shift
    o_ref[0] = _dot(h.astype(BF16), w_ref[...]).astype(BF16)


def _in_proj(x, mod, w_in):
    b, s, d = x.shape
    n = w_in.shape[1]
    tm = 512
    return pl.pallas_call(
        _in_proj_kernel,
        grid=(b, s // tm),
        in_specs=[
            pl.BlockSpec((1, tm, d), lambda i, j: (i, j, 0)),
            pl.BlockSpec((1, N_MOD, d), lambda i, j: (i, 0, 0)),
            pl.BlockSpec((d, n), lambda i, j: (0, 0)),
        ],
        out_specs=pl.BlockSpec((1, tm, n), lambda i, j: (i, j, 0)),
        out_shape=jax.ShapeDtypeStruct((b, s, n), BF16),
        compiler_params=pltpu.CompilerParams(
            dimension_semantics=("arbitrary", "arbitrary"), vmem_limit_bytes=VMEM_LIMIT),
        name="in_proj",
    )(x, mod, w_in)


def _attn_a_kernel(q_ref, k_ref, v_ref, bias_ref, o_ref, s_ref):
    q0 = pl.program_id(2) * ATT_TQ
    past = N_PREV_CHUNKS * CHUNK
    start = pl.multiple_of(jnp.maximum(q0 - past, 0), LANES)
    bias_tile = jnp.maximum(past - q0, 0) // LANES
    lane = lax.broadcasted_iota(jnp.int32, (ATT_TQ, LANES), 1)
    chains = [(g, h) for g in range(ATT_GROUPS) for h in range(HEADS_PER_BLOCK)]
    vw = [v_ref[0, pl.ds(start, ATT_WIN), g * LANES:(g + 1) * LANES] for g in range(ATT_GROUPS)]
    def qk(g, h):
        qg = q_ref[0, :, g * LANES:(g + 1) * LANES] * QK_SCALE
        kw = k_ref[0, pl.ds(start, ATT_WIN), g * LANES:(g + 1) * LANES]
        in_head = (lane >= h * HEAD_DIM) & (lane < (h + 1) * HEAD_DIM)
        return _dot_nt(jnp.where(in_head, qg, jnp.zeros_like(qg)), kw)

    outs = []
    for i, (g, h) in enumerate(chains):
        hd = g * HEADS_PER_BLOCK + h
        bias = jnp.concatenate(
            [bias_ref[hd, bias_tile + t] for t in range(ATT_WIN // LANES)], axis=1)
        s_ref[i] = qk(g, h) + bias
    for i, (g, h) in enumerate(chains):
        m = jnp.max(s_ref[i], axis=-1, keepdims=True)
        p = jnp.exp(s_ref[i] - m)
        denom = jnp.sum(p, axis=-1, keepdims=True)
        outs.append(_dot(p.astype(BF16), vw[g]) / denom)
    for g in range(ATT_GROUPS):
        o_ref[0, :, g * LANES:(g + 1) * LANES] = jnp.where(
            lane < HEAD_DIM, outs[g * HEADS_PER_BLOCK], outs[g * HEADS_PER_BLOCK + 1]
        ).astype(BF16)


def _attn_a(proj, bias):
    b, s, _ = proj.shape
    width = ATT_GROUPS * LANES
    blocks = W_A // width
    heads = ATT_GROUPS * HEADS_PER_BLOCK
    return pl.pallas_call(
        _attn_a_kernel,
        grid=(b, blocks, s // ATT_TQ),
        in_specs=[
            pl.BlockSpec((1, ATT_TQ, width), lambda i, hp, j: (i, j, hp)),
            pl.BlockSpec((1, s, width), lambda i, hp, j: (i, 0, blocks + hp)),
            pl.BlockSpec((1, s, width), lambda i, hp, j: (i, 0, 2 * blocks + hp)),
            pl.BlockSpec((heads, ATT_BIAS_TILES, ATT_TQ, LANES),
                         lambda i, hp, j: (hp, 0, 0, 0)),
        ],
        out_specs=pl.BlockSpec((1, ATT_TQ, width), lambda i, hp, j: (i, j, hp)),
        out_shape=jax.ShapeDtypeStruct((b, s, W_A), BF16),
        scratch_shapes=[pltpu.VMEM((heads, ATT_TQ, ATT_WIN), F32)],
        compiler_params=pltpu.CompilerParams(
            dimension_semantics=("arbitrary", "arbitrary", "arbitrary"),
            vmem_limit_bytes=VMEM_LIMIT),
        name="chunked_attn",
    )(proj, proj, proj, bias)


def _sb_kernel(q_ref, k_ref, v_ref, o_ref, acc_ref, carry_ref, s_ref):
    qi = pl.program_id(2)
    lane = lax.broadcasted_iota(jnp.int32, (SB_T, LANES), 1)

    def below_diagonal(rows, cols, row0):
        r = lax.broadcasted_iota(jnp.int32, (rows, cols), 0) + row0
        return r > lax.broadcasted_iota(jnp.int32, (rows, cols), 1)

    later = {n: jnp.where(below_diagonal(n, n, 0), 1.0, 0.0).astype(BF16)
             for n in (SB_R, SB_T)}

    chains = [(g, h, r) for g in range(SB_GROUPS) for h in range(HEADS_PER_BLOCK)
              for r in range(SB_T // SB_R)]
    qm = {}
    for g in range(SB_GROUPS):
        qg = q_ref[0, :, g * LANES:(g + 1) * LANES] * QK_SCALE
        for h in range(HEADS_PER_BLOCK):
            in_head = (lane >= h * HEAD_DIM) & (lane < (h + 1) * HEAD_DIM)
            qm[g, h] = jnp.where(in_head, qg, jnp.zeros_like(qg))

    def tile(t, diag):
        ks = pl.multiple_of(t * SB_T, SB_T)
        nkeys = [SB_R * (r + 1) if diag else SB_T for _, _, r in chains]

        def keys(ref, g, n):
            return ref[0, pl.ds(ks, n), g * LANES:(g + 1) * LANES]

        for i, (g, h, r) in enumerate(chains):
            s_ref[i, :, :nkeys[i]] = _dot_nt(qm[g, h][r * SB_R:(r + 1) * SB_R],
                                             keys(k_ref, g, nkeys[i]))
        carry_scales = []
        slowest = None
        for i, (g, h, r) in enumerate(chains):
            rows = slice(r * SB_R, (r + 1) * SB_R)
            hd = g * HEADS_PER_BLOCK + h
            nk = nkeys[i]
            s = s_ref[i, :, :nk]
            if diag:
                s = jnp.where(below_diagonal(SB_R, nk, r * SB_R), s, MASKED)
            log_beta = jnp.minimum(s, 0.0) - jnp.log(1.0 + jnp.exp2(jnp.abs(s) * NEG_LOG2E))
            log_keep = log_beta - s
            s_ref[i, :, :nk] = log_beta + _dot(log_keep.astype(BF16), later[nk])
            carry = jnp.sum(log_keep, axis=-1, keepdims=True)
            if not diag:
                before = carry_ref[hd, rows, :]
                carry_scales.append(jnp.exp(before))
                carry = carry + before
            carry_ref[hd, rows, :] = carry
            slowest = carry if slowest is None else jnp.maximum(slowest, carry)
        for i, (g, h, r) in enumerate(chains):
            rows = slice(r * SB_R, (r + 1) * SB_R)
            hd = g * HEADS_PER_BLOCK + h
            w = jnp.exp(s_ref[i, :, :nkeys[i]])
            pv = _dot(w.astype(BF16), keys(v_ref, g, nkeys[i]))
            if diag:
                acc_ref[hd, rows, :] = pv
            else:
                acc_ref[hd, rows, :] += carry_scales[i] * pv
        return jnp.max(slowest) > SB_DEAD

    alive = tile(qi, True)

    def cond(state):
        t, alive = state
        return jnp.logical_and(t >= 0, alive)

    def body(state):
        t, _ = state
        return t - 1, tile(t, False)

    lax.while_loop(cond, body, (qi - 1, alive))
    for g in range(SB_GROUPS):
        o_ref[0, :, g * LANES:(g + 1) * LANES] = jnp.where(
            lane < HEAD_DIM, acc_ref[g * HEADS_PER_BLOCK], acc_ref[g * HEADS_PER_BLOCK + 1]
        ).astype(BF16)


def _attn_sb(proj):
    b, s, _ = proj.shape
    width = SB_GROUPS * LANES
    blocks = W_B // width
    base = 3 * W_A // width
    heads = SB_GROUPS * HEADS_PER_BLOCK
    return pl.pallas_call(
        _sb_kernel,
        grid=(b, blocks, s // SB_T),
        in_specs=[
            pl.BlockSpec((1, SB_T, width), lambda i, hp, j: (i, j, base + hp)),
            pl.BlockSpec((1, s, width), lambda i, hp, j: (i, 0, base + blocks + hp)),
            pl.BlockSpec((1, s, width), lambda i, hp, j: (i, 0, base + 2 * blocks + hp)),
        ],
        out_specs=pl.BlockSpec((1, SB_T, width), lambda i, hp, j: (i, j, hp)),
        out_shape=jax.ShapeDtypeStruct((b, s, W_B), BF16),
        scratch_shapes=[
            pltpu.VMEM((heads, SB_T, LANES), F32),
            pltpu.VMEM((heads, SB_T, 1), F32),
            pltpu.VMEM((heads * (SB_T // SB_R), SB_R, SB_T), F32),
        ],
        compiler_params=pltpu.CompilerParams(
            dimension_semantics=("arbitrary", "arbitrary", "arbitrary"),
            vmem_limit_bytes=VMEM_LIMIT),
        name="stick_breaking_attn",
    )(proj, proj, proj)


def _ffn_kernel(x_ref, oa_ref, ob_ref, mod_ref, ga_ref, gb_ref, wout_ref, wup_ref,
                cw_ref, cb_ref, wdown_ref, fg_ref, o_ref, act_ref, tail_ref, *, final):
    tm = FFN_TM

    @pl.when(pl.program_id(1) == 0)
    def _():
        tail_ref[...] = jnp.zeros_like(tail_ref)

    gate_mix = mod_ref[0, 2:3, :]
    shift = mod_ref[0, 3:4, :]
    scale = mod_ref[0, 4:5, :]
    gate_ffn = mod_ref[0, 5:6, :]

    na = _rms(oa_ref[0].astype(F32)) * ga_ref[...]
    nb = _rms(ob_ref[0].astype(F32)) * gb_ref[...]
    mix_in = jnp.concatenate([na, nb], axis=-1).astype(BF16)
    x1 = x_ref[0] + gate_mix * _dot(mix_in, wout_ref[...])

    h = (_rms(x1) * (1.0 + scale) + shift).astype(BF16)
    row = lax.broadcasted_iota(jnp.int32, (tm, FFN_CHUNK), 0)

    def conv(cols):
        up = _dot(h, wup_ref[:, cols])
        prev = tail_ref[:, cols]
        tail_ref[:, cols] = up[tm - 8:, :]
        m1 = jnp.where(row == 0, prev[7:8, :], pltpu.roll(up, 1, axis=0))
        m2 = jnp.where(row == 0, prev[6:7, :],
                       jnp.where(row == 1, prev[7:8, :], pltpu.roll(up, 2, axis=0)))
        return (cw_ref[0:1, cols] * m2 + cw_ref[1:2, cols] * m1
                + cw_ref[2:3, cols] * up + cb_ref[:, cols])

    for j in range(D_FF // FFN_CHUNK):
        g = conv(slice(j * FFN_CHUNK, (j + 1) * FFN_CHUNK))
        v = conv(slice(D_FF + j * FFN_CHUNK, D_FF + (j + 1) * FFN_CHUNK))
        act = g / (1.0 + jnp.exp(-g)) * v
        act_ref[:, j * FFN_CHUNK:(j + 1) * FFN_CHUNK] = act.astype(BF16)

    x2 = x1 + gate_ffn * _dot(act_ref[...], wdown_ref[...])
    if final:
        x2 = _rms(x2) * fg_ref[...]
    o_ref[0] = x2


def _ffn(x, oa, ob, mod, g_a, g_b, w_out, w_up, conv_w, conv_b, w_down, final_g, final):
    b, s, d = x.shape
    tm = FFN_TM
    const = lambda i, j: (0, 0)
    tok = lambda i, j: (i, j, 0)
    return pl.pallas_call(
        functools.partial(_ffn_kernel, final=final),
        grid=(b, s // tm),
        in_specs=[
            pl.BlockSpec((1, tm, d), tok),
            pl.BlockSpec((1, tm, W_A), tok),
            pl.BlockSpec((1, tm, W_B), tok),
            pl.BlockSpec((1, N_MOD, d), lambda i, j: (i, 0, 0)),
            pl.BlockSpec((1, W_A), const),
            pl.BlockSpec((1, W_B), const),
            pl.BlockSpec((W_A + W_B, d), const, pipeline_mode=pl.Buffered(1)),
            pl.BlockSpec((d, 2 * D_FF), const, pipeline_mode=pl.Buffered(1)),
            pl.BlockSpec((3, 2 * D_FF), const),
            pl.BlockSpec((1, 2 * D_FF), const),
            pl.BlockSpec((D_FF, d), const, pipeline_mode=pl.Buffered(1)),
            pl.BlockSpec((1, d), const),
        ],
        out_specs=pl.BlockSpec((1, tm, d), tok),
        out_shape=jax.ShapeDtypeStruct((b, s, d), F32),
        scratch_shapes=[
            pltpu.VMEM((tm, D_FF), BF16),
            pltpu.VMEM((8, 2 * D_FF), F32),
        ],
        compiler_params=pltpu.CompilerParams(
            dimension_semantics=("arbitrary", "arbitrary"), vmem_limit_bytes=VMEM_LIMIT),
        name="out_proj_ffn",
    )(x, oa, ob, mod, g_a, g_b, w_out, w_up, conv_w, conv_b, w_down, final_g)


def kernel(x, c, w_ada, b_ada, w_in, rel_bias, g_a, g_b, w_out, w_up, conv_w, conv_b,
           w_down, final_g):
    depth = w_ada.shape[0]
    b = x.shape[0]
    mod = _modulation(c, w_ada, b_ada).reshape(depth, b, N_MOD, D_MODEL)
    bias = _bias_tables(rel_bias).reshape(depth, H_A, ATT_BIAS_TILES, ATT_TQ, LANES)
    for l in range(depth):
        proj = _in_proj(x, mod[l], w_in[l].astype(BF16))
        oa = _attn_a(proj, bias[l])
        ob = _attn_sb(proj)
        x = _ffn(x, oa, ob, mod[l], g_a[l][None], g_b[l][None], w_out[l].astype(BF16),
                 w_up[l].astype(BF16), conv_w[l], conv_b[l][None], w_down[l].astype(BF16),
                 final_g[None], final=(l == depth - 1))
    return x
```

```python
import functools

import jax
import jax.numpy as jnp
from jax import lax
from jax.experimental import pallas as pl
from jax.experimental.pallas import tpu as pltpu

F32 = jnp.float32
BF16 = jnp.bfloat16

D_MODEL = 1024
CHUNK = 64
N_PREV_CHUNKS = 8
HEAD_DIM = 64
H_A = 8
H_B = 8
W_A = H_A * HEAD_DIM
W_B = H_B * HEAD_DIM
W_IN = 3 * (W_A + W_B)
REL_CLIP = 128
D_FF = 2816
N_MOD = 6
EPS = 1e-6
QK_SCALE = HEAD_DIM ** -0.5

LANES = 128
HEADS_PER_BLOCK = LANES // HEAD_DIM
MASKED = -1e30
NEG_LOG2E = -1.4426950408889634

ATT_TQ = 128
ATT_WIN = N_PREV_CHUNKS * CHUNK + ATT_TQ
ATT_BIAS_TILES = (N_PREV_CHUNKS * CHUNK + ATT_WIN) // LANES

SB_T = 256
SB_R = 128
SB_GROUPS = 2
SB_DEAD = -104.0

FFN_TM = 512
FFN_CHUNK = 256
VMEM_LIMIT = 56 * 1024 * 1024


def _dot(a, b):
    return jnp.dot(a, b, preferred_element_type=F32)


def _dot_nt(a, b):
    return lax.dot_general(a, b, (((1,), (1,)), ((), ())), preferred_element_type=F32)


def _rms(x):
    return x * lax.rsqrt(jnp.mean(x * x, axis=-1, keepdims=True) + EPS)


def _split_bf16(x):
    hi = x.astype(BF16)
    lo = (x - hi.astype(F32)).astype(BF16)
    return hi, lo


def _mod_kernel(c_ref, w_ref, b_ref, o_ref):
    c = c_ref[...]
    ca = c / (1.0 + jnp.exp(-c))
    a_hi, a_lo = _split_bf16(ca)
    w_hi, w_lo = _split_bf16(w_ref[0])
    acc = _dot(a_hi, w_hi) + _dot(a_hi, w_lo) + _dot(a_lo, w_hi)
    o_ref[0] = acc + b_ref[0]


def _modulation(c, w_ada, b_ada):
    depth, d, n = w_ada.shape
    b = c.shape[0]
    tn = 1536
    return pl.pallas_call(
        _mod_kernel,
        grid=(depth, n // tn),
        in_specs=[
            pl.BlockSpec((b, d), lambda l, j: (0, 0)),
            pl.BlockSpec((1, d, tn), lambda l, j: (l, 0, j)),
            pl.BlockSpec((1, 1, tn), lambda l, j: (l, 0, j)),
        ],
        out_specs=pl.BlockSpec((1, b, tn), lambda l, j: (l, 0, j)),
        out_shape=jax.ShapeDtypeStruct((depth, b, n), F32),
        compiler_params=pltpu.CompilerParams(
            dimension_semantics=("arbitrary", "arbitrary"), vmem_limit_bytes=VMEM_LIMIT),
        name="adaln_mod",
    )(c, w_ada, b_ada.reshape(depth, 1, n))


def _bias_kernel(rb_ref, o_ref):
    lh = pl.program_id(0)
    i = lax.broadcasted_iota(jnp.int32, (ATT_TQ, LANES), 0)
    past = N_PREV_CHUNKS * CHUNK
    for ct in range(ATT_BIAS_TILES):
        if (ct + 1) * LANES > ATT_WIN:
            o_ref[0, ct] = jnp.full((ATT_TQ, LANES), MASKED, F32)
            continue
        r = lax.broadcasted_iota(jnp.int32, (ATT_TQ, LANES), 1) + ct * LANES
        dist = jnp.clip(past + i - r, -REL_CLIP, REL_CLIP) + REL_CLIP
        d_lo = min(max(past - (ct + 1) * LANES + 1, -REL_CLIP), REL_CLIP) + REL_CLIP
        d_hi = min(max(past + ATT_TQ - 1 - ct * LANES, -REL_CLIP), REL_CLIP) + REL_CLIP

        def body(d, acc, dist=dist):
            return jnp.where(dist == d, rb_ref[lh, d], acc)

        acc = lax.fori_loop(d_lo, d_hi + 1, body, jnp.zeros((ATT_TQ, LANES), F32),
                            unroll=8)
        qc = i >> 6
        kc = r >> 6
        visible = (kc >= qc) & (kc <= qc + N_PREV_CHUNKS)
        o_ref[0, ct] = jnp.where(visible, acc, MASKED)


def _bias_tables(rel_bias):
    depth, h, nrel = rel_bias.shape
    return pl.pallas_call(
        _bias_kernel,
        grid=(depth * h,),
        in_specs=[pl.BlockSpec(memory_space=pltpu.SMEM)],
        out_specs=pl.BlockSpec((1, ATT_BIAS_TILES, ATT_TQ, LANES), lambda n: (n, 0, 0, 0)),
        out_shape=jax.ShapeDtypeStruct((depth * h, ATT_BIAS_TILES, ATT_TQ, LANES), F32),
        compiler_params=pltpu.CompilerParams(dimension_semantics=("arbitrary",)),
        name="rel_bias_table",
    )(rel_bias.reshape(depth * h, nrel))


def _in_proj_kernel(x_ref, mod_ref, w_ref, o_ref):
    x = x_ref[0]
    shift = mod_ref[0, 0:1, :]
    scale = mod_ref[0, 1:2, :]
    h = _rms(x) * (1.0 + scale) + shift
    o_ref[0] = _dot(h.astype(BF16), w_ref[...]).astype(BF16)


def _in_proj(x, mod, w_in):
    b, s, d = x.shape
    n = w_in.shape[1]
    tm = 512
    return pl.pallas_call(
        _in_proj_kernel,
        grid=(b, s // tm),
        in_specs=[
            pl.BlockSpec((1, tm, d), lambda i, j: (i, j, 0)),
            pl.BlockSpec((1, N_MOD, d), lambda i, j: (i, 0, 0)),
            pl.BlockSpec((d, n), lambda i, j: (0, 0)),
        ],
        out_specs=pl.BlockSpec((1, tm, n), lambda i, j: (i, j, 0)),
        out_shape=jax.ShapeDtypeStruct((b, s, n), BF16),
        compiler_params=pltpu.CompilerParams(
            dimension_semantics=("arbitrary", "arbitrary"), vmem_limit_bytes=VMEM_LIMIT),
        name="in_proj",
    )(x, mod, w_in)


def _chunked_block(q_ref, k_ref, v_ref, bias_ref, o_ref, s_ref, q0_block, row_chunks, groups):
    past = N_PREV_CHUNKS * CHUNK
    lane = lax.broadcasted_iota(jnp.int32, (ATT_TQ, LANES), 1)
    chains = [(c, g, h) for c in range(row_chunks) for g in range(groups)
              for h in range(HEADS_PER_BLOCK)]
    start, bias_tile = [], []
    for c in range(row_chunks):
        q0 = q0_block + c * ATT_TQ
        start.append(pl.multiple_of(jnp.maximum(q0 - past, 0), LANES))
        bias_tile.append(jnp.maximum(past - q0, 0) // LANES)

    def window(ref, c, g):
        return ref[0, pl.ds(start[c], ATT_WIN), g * LANES:(g + 1) * LANES]

    for i, (c, g, h) in enumerate(chains):
        qg = q_ref[0, c * ATT_TQ:(c + 1) * ATT_TQ, g * LANES:(g + 1) * LANES] * QK_SCALE
        in_head = (lane >= h * HEAD_DIM) & (lane < (h + 1) * HEAD_DIM)
        bias = jnp.concatenate(
            [bias_ref[g * HEADS_PER_BLOCK + h, bias_tile[c] + t]
             for t in range(ATT_WIN // LANES)], axis=1)
        s_ref[i] = _dot_nt(jnp.where(in_head, qg, jnp.zeros_like(qg)), window(k_ref, c, g)) + bias
    outs = {}
    for i, (c, g, h) in enumerate(chains):
        m = jnp.max(s_ref[i], axis=-1, keepdims=True)
        p = jnp.exp(s_ref[i] - m)
        denom = jnp.sum(p, axis=-1, keepdims=True)
        outs[c, g, h] = _dot(p.astype(BF16), window(v_ref, c, g)) / denom
    for c in range(row_chunks):
        for g in range(groups):
            o_ref[0, c * ATT_TQ:(c + 1) * ATT_TQ, g * LANES:(g + 1) * LANES] = jnp.where(
                lane < HEAD_DIM, outs[c, g, 0], outs[c, g, 1]).astype(BF16)


def _mixers_kernel(qa_ref, ka_ref, va_ref, bias_ref, q_ref, k_ref, v_ref, oa_ref, o_ref,
                   acc_ref, carry_ref, s_ref, sa_ref):
    qi = pl.program_id(2)
    _chunked_block(qa_ref, ka_ref, va_ref, bias_ref, oa_ref, sa_ref,
                   qi * SB_T, SB_T // ATT_TQ, SB_GROUPS)
    lane = lax.broadcasted_iota(jnp.int32, (SB_T, LANES), 1)

    def below_diagonal(rows, cols, row0):
        r = lax.broadcasted_iota(jnp.int32, (rows, cols), 0) + row0
        return r > lax.broadcasted_iota(jnp.int32, (rows, cols), 1)

    later = {n: jnp.where(below_diagonal(n, n, 0), 1.0, 0.0).astype(BF16)
             for n in (SB_R, SB_T)}

    chains = [(g, h, r) for g in range(SB_GROUPS) for h in range(HEADS_PER_BLOCK)
              for r in range(SB_T // SB_R)]
    qm = {}
    for g in range(SB_GROUPS):
        qg = q_ref[0, :, g * LANES:(g + 1) * LANES] * QK_SCALE
        for h in range(HEADS_PER_BLOCK):
            in_head = (lane >= h * HEAD_DIM) & (lane < (h + 1) * HEAD_DIM)
            qm[g, h] = jnp.where(in_head, qg, jnp.zeros_like(qg))

    def tile(t, diag):
        ks = pl.multiple_of(t * SB_T, SB_T)
        nkeys = [SB_R * (r + 1) if diag else SB_T for _, _, r in chains]

        def keys(ref, g, n):
            return ref[0, pl.ds(ks, n), g * LANES:(g + 1) * LANES]

        for i, (g, h, r) in enumerate(chains):
            s_ref[i, :, :nkeys[i]] = _dot_nt(qm[g, h][r * SB_R:(r + 1) * SB_R],
                                             keys(k_ref, g, nkeys[i]))
        carry_scales = []
        slowest = None
        for i, (g, h, r) in enumerate(chains):
            rows = slice(r * SB_R, (r + 1) * SB_R)
            hd = g * HEADS_PER_BLOCK + h
            nk = nkeys[i]
            s = s_ref[i, :, :nk]
            if diag:
                s = jnp.where(below_diagonal(SB_R, nk, r * SB_R), s, MASKED)
            log_beta = jnp.minimum(s, 0.0) - jnp.log(1.0 + jnp.exp2(jnp.abs(s) * NEG_LOG2E))
            log_keep = log_beta - s
            s_ref[i, :, :nk] = log_beta + _dot(log_keep.astype(BF16), later[nk])
            carry = jnp.sum(log_keep, axis=-1, keepdims=True)
            if not diag:
                before = carry_ref[hd, rows, :]
                carry_scales.append(jnp.exp(before))
                carry = carry + before
            carry_ref[hd, rows, :] = carry
            slowest = carry if slowest is None else jnp.maximum(slowest, carry)
        for i, (g, h, r) in enumerate(chains):
            rows = slice(r * SB_R, (r + 1) * SB_R)
            hd = g * HEADS_PER_BLOCK + h
            w = jnp.exp(s_ref[i, :, :nkeys[i]])
            pv = _dot(w.astype(BF16), keys(v_ref, g, nkeys[i]))
            if diag:
                acc_ref[hd, rows, :] = pv
            else:
                acc_ref[hd, rows, :] += carry_scales[i] * pv
        return jnp.max(slowest) > SB_DEAD

    alive = tile(qi, True)

    def cond(state):
        t, alive = state
        return jnp.logical_and(t >= 0, alive)

    def body(state):
        t, _ = state
        return t - 1, tile(t, False)

    lax.while_loop(cond, body, (qi - 1, alive))
    for g in range(SB_GROUPS):
        o_ref[0, :, g * LANES:(g + 1) * LANES] = jnp.where(
            lane < HEAD_DIM, acc_ref[g * HEADS_PER_BLOCK], acc_ref[g * HEADS_PER_BLOCK + 1]
        ).astype(BF16)


def _mixers(proj, bias):
    b, s, _ = proj.shape
    width = SB_GROUPS * LANES
    blocks = W_B // width
    base = 3 * W_A // width
    heads = SB_GROUPS * HEADS_PER_BLOCK
    rows = lambda col: pl.BlockSpec((1, SB_T, width), lambda i, hp, j: (i, j, col + hp))
    full = lambda col: pl.BlockSpec((1, s, width), lambda i, hp, j: (i, 0, col + hp))
    out = pl.BlockSpec((1, SB_T, width), lambda i, hp, j: (i, j, hp))
    return pl.pallas_call(
        _mixers_kernel,
        grid=(b, blocks, s // SB_T),
        in_specs=[
            rows(0), full(blocks), full(2 * blocks),
            pl.BlockSpec((heads, ATT_BIAS_TILES, ATT_TQ, LANES), lambda i, hp, j: (hp, 0, 0, 0)),
            rows(base), full(base + blocks), full(base + 2 * blocks),
        ],
        out_specs=[out, out],
        out_shape=[jax.ShapeDtypeStruct((b, s, W_A), BF16),
                   jax.ShapeDtypeStruct((b, s, W_B), BF16)],
        scratch_shapes=[
            pltpu.VMEM((heads, SB_T, LANES), F32),
            pltpu.VMEM((heads, SB_T, 1), F32),
            pltpu.VMEM((heads * (SB_T // SB_R), SB_R, SB_T), F32),
            pltpu.VMEM((heads * (SB_T // ATT_TQ), ATT_TQ, ATT_WIN), F32),
        ],
        compiler_params=pltpu.CompilerParams(
            dimension_semantics=("arbitrary", "arbitrary", "arbitrary"),
            vmem_limit_bytes=VMEM_LIMIT),
        name="token_mixers",
    )(proj, proj, proj, bias, proj, proj, proj)


def _ffn_kernel(x_ref, oa_ref, ob_ref, mod_ref, ga_ref, gb_ref, wout_ref, wup_ref,
                cw_ref, cb_ref, wdown_ref, fg_ref, o_ref, act_ref, tail_ref, *, final):
    tm = FFN_TM

    @pl.when(pl.program_id(1) == 0)
    def _():
        tail_ref[...] = jnp.zeros_like(tail_ref)

    gate_mix = mod_ref[0, 2:3, :]
    shift = mod_ref[0, 3:4, :]
    scale = mod_ref[0, 4:5, :]
    gate_ffn = mod_ref[0, 5:6, :]

    na = _rms(oa_ref[0].astype(F32)) * ga_ref[...]
    nb = _rms(ob_ref[0].astype(F32)) * gb_ref[...]
    mix_in = jnp.concatenate([na, nb], axis=-1).astype(BF16)
    x1 = x_ref[0] + gate_mix * _dot(mix_in, wout_ref[...])

    h = (_rms(x1) * (1.0 + scale) + shift).astype(BF16)
    row = lax.broadcasted_iota(jnp.int32, (tm, FFN_CHUNK), 0)

    def conv(cols):
        up = _dot(h, wup_ref[:, cols])
        prev = tail_ref[:, cols]
        tail_ref[:, cols] = up[tm - 8:, :]
        m1 = jnp.where(row == 0, prev[7:8, :], pltpu.roll(up, 1, axis=0))
        m2 = jnp.where(row == 0, prev[6:7, :],
                       jnp.where(row == 1, prev[7:8, :], pltpu.roll(up, 2, axis=0)))
        return (cw_ref[0:1, cols] * m2 + cw_ref[1:2, cols] * m1
                + cw_ref[2:3, cols] * up + cb_ref[:, cols])

    for j in range(D_FF // FFN_CHUNK):
        g = conv(slice(j * FFN_CHUNK, (j + 1) * FFN_CHUNK))
        v = conv(slice(D_FF + j * FFN_CHUNK, D_FF + (j + 1) * FFN_CHUNK))
        act = g / (1.0 + jnp.exp(-g)) * v
        act_ref[:, j * FFN_CHUNK:(j + 1) * FFN_CHUNK] = act.astype(BF16)

    x2 = x1 + gate_ffn * _dot(act_ref[...], wdown_ref[...])
    if final:
        x2 = _rms(x2) * fg_ref[...]
    o_ref[0] = x2


def _ffn(x, oa, ob, mod, g_a, g_b, w_out, w_up, conv_w, conv_b, w_down, final_g, final):
    b, s, d = x.shape
    tm = FFN_TM
    const = lambda i, j: (0, 0)
    tok = lambda i, j: (i, j, 0)
    return pl.pallas_call(
        functools.partial(_ffn_kernel, final=final),
        grid=(b, s // tm),
        in_specs=[
            pl.BlockSpec((1, tm, d), tok),
            pl.BlockSpec((1, tm, W_A), tok),
            pl.BlockSpec((1, tm, W_B), tok),
            pl.BlockSpec((1, N_MOD, d), lambda i, j: (i, 0, 0)),
            pl.BlockSpec((1, W_A), const),
            pl.BlockSpec((1, W_B), const),
            pl.BlockSpec((W_A + W_B, d), const, pipeline_mode=pl.Buffered(1)),
            pl.BlockSpec((d, 2 * D_FF), const, pipeline_mode=pl.Buffered(1)),
            pl.BlockSpec((3, 2 * D_FF), const),
            pl.BlockSpec((1, 2 * D_FF), const),
            pl.BlockSpec((D_FF, d), const, pipeline_mode=pl.Buffered(1)),
            pl.BlockSpec((1, d), const),
        ],
        out_specs=pl.BlockSpec((1, tm, d), tok),
        out_shape=jax.ShapeDtypeStruct((b, s, d), F32),
        scratch_shapes=[
            pltpu.VMEM((tm, D_FF), BF16),
            pltpu.VMEM((8, 2 * D_FF), F32),
        ],
        compiler_params=pltpu.CompilerParams(
            dimension_semantics=("arbitrary", "arbitrary"), vmem_limit_bytes=VMEM_LIMIT),
        name="out_proj_ffn",
    )(x, oa, ob, mod, g_a, g_b, w_out, w_up, conv_w, conv_b, w_down, final_g)


def kernel(x, c, w_ada, b_ada, w_in, rel_bias, g_a, g_b, w_out, w_up, conv_w, conv_b,
           w_down, final_g):
    depth = w_ada.shape[0]
    b = x.shape[0]
    mod = _modulation(c, w_ada, b_ada).reshape(depth, b, N_MOD, D_MODEL)
    bias = _bias_tables(rel_bias).reshape(depth, H_A, ATT_BIAS_TILES, ATT_TQ, LANES)
    for l in range(depth):
        proj = _in_proj(x, mod[l], w_in[l].astype(BF16))
        oa, ob = _mixers(proj, bias[l])
        x = _ffn(x, oa, ob, mod[l], g_a[l][None], g_b[l][None], w_out[l].astype(BF16),
                 w_up[l].astype(BF16), conv_w[l], conv_b[l][None], w_down[l].astype(BF16),
                 final_g[None], final=(l == depth - 1))
    return x
```

```python
import functools

import jax
import jax.numpy as jnp
from jax import lax
from jax.experimental import pallas as pl
from jax.experimental.pallas import tpu as pltpu

F32 = jnp.float32
BF16 = jnp.bfloat16

D_MODEL = 1024
CHUNK = 64
N_PREV_CHUNKS = 8
HEAD_DIM = 64
H_A = 8
H_B = 8
W_A = H_A * HEAD_DIM
W_B = H_B * HEAD_DIM
W_IN = 3 * (W_A + W_B)
REL_CLIP = 128
D_FF = 2816
N_MOD = 6
EPS = 1e-6
QK_SCALE = HEAD_DIM ** -0.5

LANES = 128
HEADS_PER_BLOCK = LANES // HEAD_DIM
MASKED = -1e30
NEG_LOG2E = -1.4426950408889634

ATT_TQ = 128
ATT_WIN = N_PREV_CHUNKS * CHUNK + ATT_TQ
ATT_BIAS_TILES = (N_PREV_CHUNKS * CHUNK + ATT_WIN) // LANES

SB_T = 256
SB_R = 128
SB_GROUPS = 2
SB_DEAD = -104.0

FFN_TM = 512
FFN_CHUNK = 256
VMEM_LIMIT = 56 * 1024 * 1024


def _dot(a, b):
    return jnp.dot(a, b, preferred_element_type=F32)


def _dot_nt(a, b):
    return lax.dot_general(a, b, (((1,), (1,)), ((), ())), preferred_element_type=F32)


def _rms(x):
    return x * lax.rsqrt(jnp.mean(x * x, axis=-1, keepdims=True) + EPS)


def _split_bf16(x):
    hi = x.astype(BF16)
    lo = (x - hi.astype(F32)).astype(BF16)
    return hi, lo


def _mod_kernel(c_ref, w_ref, b_ref, o_ref):
    c = c_ref[...]
    ca = c / (1.0 + jnp.exp(-c))
    a_hi, a_lo = _split_bf16(ca)
    w_hi, w_lo = _split_bf16(w_ref[0])
    acc = _dot(a_hi, w_hi) + _dot(a_hi, w_lo) + _dot(a_lo, w_hi)
    o_ref[0] = acc + b_ref[0]


def _modulation(c, w_ada, b_ada):
    depth, d, n = w_ada.shape
    b = c.shape[0]
    tn = 1536
    return pl.pallas_call(
        _mod_kernel,
        grid=(depth, n // tn),
        in_specs=[
            pl.BlockSpec((b, d), lambda l, j: (0, 0)),
            pl.BlockSpec((1, d, tn), lambda l, j: (l, 0, j)),
            pl.BlockSpec((1, 1, tn), lambda l, j: (l, 0, j)),
        ],
        out_specs=pl.BlockSpec((1, b, tn), lambda l, j: (l, 0, j)),
        out_shape=jax.ShapeDtypeStruct((depth, b, n), F32),
        compiler_params=pltpu.CompilerParams(
            dimension_semantics=("arbitrary", "arbitrary"), vmem_limit_bytes=VMEM_LIMIT),
        name="adaln_mod",
    )(c, w_ada, b_ada.reshape(depth, 1, n))


def _bias_kernel(rb_ref, o_ref):
    lh = pl.program_id(0)
    i = lax.broadcasted_iota(jnp.int32, (ATT_TQ, LANES), 0)
    past = N_PREV_CHUNKS * CHUNK
    for ct in range(ATT_BIAS_TILES):
        if (ct + 1) * LANES > ATT_WIN:
            o_ref[0, ct] = jnp.full((ATT_TQ, LANES), MASKED, F32)
            continue
        r = lax.broadcasted_iota(jnp.int32, (ATT_TQ, LANES), 1) + ct * LANES
        dist = jnp.clip(past + i - r, -REL_CLIP, REL_CLIP) + REL_CLIP
        d_lo = min(max(past - (ct + 1) * LANES + 1, -REL_CLIP), REL_CLIP) + REL_CLIP
        d_hi = min(max(past + ATT_TQ - 1 - ct * LANES, -REL_CLIP), REL_CLIP) + REL_CLIP

        def body(d, acc, dist=dist):
            return jnp.where(dist == d, rb_ref[lh, d], acc)

        acc = lax.fori_loop(d_lo, d_hi + 1, body, jnp.zeros((ATT_TQ, LANES), F32),
                            unroll=8)
        qc = i >> 6
        kc = r >> 6
        visible = (kc >= qc) & (kc <= qc + N_PREV_CHUNKS)
        o_ref[0, ct] = jnp.where(visible, acc, MASKED)


def _bias_tables(rel_bias):
    depth, h, nrel = rel_bias.shape
    return pl.pallas_call(
        _bias_kernel,
        grid=(depth * h,),
        in_specs=[pl.BlockSpec(memory_space=pltpu.SMEM)],
        out_specs=pl.BlockSpec((1, ATT_BIAS_TILES, ATT_TQ, LANES), lambda n: (n, 0, 0, 0)),
        out_shape=jax.ShapeDtypeStruct((depth * h, ATT_BIAS_TILES, ATT_TQ, LANES), F32),
        compiler_params=pltpu.CompilerParams(dimension_semantics=("arbitrary",)),
        name="rel_bias_table",
    )(rel_bias.reshape(depth * h, nrel))


def _in_proj_kernel(x_ref, mod_ref, w_ref, o_ref):
    x = x_ref[0]
    shift = mod_ref[0, 0:1, :]
    scale = mod_ref[0, 1:2, :]
    h = _rms(x) * (1.0 + scale) + shift
    o_ref[0] = _dot(h.astype(BF16), w_ref[...]).astype(BF16)


def _in_proj(x, mod, w_in):
    b, s, d = x.shape
    n = w_in.shape[1]
    tm = 1024
    return pl.pallas_call(
        _in_proj_kernel,
        grid=(b, s // tm),
        in_specs=[
            pl.BlockSpec((1, tm, d), lambda i, j: (i, j, 0)),
            pl.BlockSpec((1, N_MOD, d), lambda i, j: (i, 0, 0)),
            pl.BlockSpec((d, n), lambda i, j: (0, 0)),
        ],
        out_specs=pl.BlockSpec((1, tm, n), lambda i, j: (i, j, 0)),
        out_shape=jax.ShapeDtypeStruct((b, s, n), BF16),
        compiler_params=pltpu.CompilerParams(
            dimension_semantics=("arbitrary", "arbitrary"), vmem_limit_bytes=VMEM_LIMIT),
        name="in_proj",
    )(x, mod, w_in)


def _chunked_block(q_ref, k_ref, v_ref, bias_ref, o_ref, s_ref, q0_block, row_chunks, groups):
    past = N_PREV_CHUNKS * CHUNK
    lane = lax.broadcasted_iota(jnp.int32, (ATT_TQ, LANES), 1)
    chains = [(c, g, h) for c in range(row_chunks) for g in range(groups)
              for h in range(HEADS_PER_BLOCK)]
    start, bias_tile = [], []
    for c in range(row_chunks):
        q0 = q0_block + c * ATT_TQ
        start.append(pl.multiple_of(jnp.maximum(q0 - past, 0), LANES))
        bias_tile.append(jnp.maximum(past - q0, 0) // LANES)

    def window(ref, c, g):
        return ref[0, pl.ds(start[c], ATT_WIN), g * LANES:(g + 1) * LANES]

    for i, (c, g, h) in enumerate(chains):
        qg = q_ref[0, c * ATT_TQ:(c + 1) * ATT_TQ, g * LANES:(g + 1) * LANES] * QK_SCALE
        in_head = (lane >= h * HEAD_DIM) & (lane < (h + 1) * HEAD_DIM)
        bias = jnp.concatenate(
            [bias_ref[g * HEADS_PER_BLOCK + h, bias_tile[c] + t]
             for t in range(ATT_WIN // LANES)], axis=1)
        s_ref[i] = _dot_nt(jnp.where(in_head, qg, jnp.zeros_like(qg)), window(k_ref, c, g)) + bias
    outs = {}
    for i, (c, g, h) in enumerate(chains):
        m = jnp.max(s_ref[i], axis=-1, keepdims=True)
        p = jnp.exp(s_ref[i] - m)
        denom = jnp.sum(p, axis=-1, keepdims=True)
        outs[c, g, h] = _dot(p.astype(BF16), window(v_ref, c, g)) / denom
    for c in range(row_chunks):
        for g in range(groups):
            o_ref[0, c * ATT_TQ:(c + 1) * ATT_TQ, g * LANES:(g + 1) * LANES] = jnp.where(
                lane < HEAD_DIM, outs[c, g, 0], outs[c, g, 1]).astype(BF16)


def _mixers_kernel(qa_ref, ka_ref, va_ref, bias_ref, q_ref, k_ref, v_ref, oa_ref, o_ref,
                   acc_ref, carry_ref, s_ref, sa_ref):
    qi = pl.program_id(2)
    _chunked_block(qa_ref, ka_ref, va_ref, bias_ref, oa_ref, sa_ref,
                   qi * SB_T, SB_T // ATT_TQ, SB_GROUPS)
    lane = lax.broadcasted_iota(jnp.int32, (SB_T, LANES), 1)

    def below_diagonal(rows, cols, row0):
        r = lax.broadcasted_iota(jnp.int32, (rows, cols), 0) + row0
        return r > lax.broadcasted_iota(jnp.int32, (rows, cols), 1)

    later = {n: jnp.where(below_diagonal(n, n, 0), 1.0, 0.0).astype(BF16)
             for n in range(LANES, SB_T + 1, LANES)}

    chains = [(g, h, r) for g in range(SB_GROUPS) for h in range(HEADS_PER_BLOCK)
              for r in range(SB_T // SB_R)]
    qm = {}
    for g in range(SB_GROUPS):
        qg = q_ref[0, :, g * LANES:(g + 1) * LANES] * QK_SCALE
        for h in range(HEADS_PER_BLOCK):
            in_head = (lane >= h * HEAD_DIM) & (lane < (h + 1) * HEAD_DIM)
            qm[g, h] = jnp.where(in_head, qg, jnp.zeros_like(qg))

    def tile(t, diag, groups):
        ks = pl.multiple_of(t * SB_T, SB_T)
        nkeys = [-(-SB_R * (r + 1) // LANES) * LANES if diag else SB_T for _, _, r in chains]
        live = [(i, c) for i, c in enumerate(chains) if c[0] in groups]

        def keys(ref, g, n):
            return ref[0, pl.ds(ks, n), g * LANES:(g + 1) * LANES]

        for i, (g, h, r) in live:
            s_ref[i, :, :nkeys[i]] = _dot_nt(qm[g, h][r * SB_R:(r + 1) * SB_R],
                                             keys(k_ref, g, nkeys[i]))
        carry_scales = {}
        slowest = {}
        for i, (g, h, r) in live:
            rows = slice(r * SB_R, (r + 1) * SB_R)
            hd = g * HEADS_PER_BLOCK + h
            nk = nkeys[i]
            s = s_ref[i, :, :nk]
            if diag:
                s = jnp.where(below_diagonal(SB_R, nk, r * SB_R), s, MASKED)
            log_beta = jnp.minimum(s, 0.0) - jnp.log(1.0 + jnp.exp2(jnp.abs(s) * NEG_LOG2E))
            log_keep = log_beta - s
            s_ref[i, :, :nk] = log_beta + _dot(log_keep.astype(BF16), later[nk])
            carry = jnp.sum(log_keep, axis=-1, keepdims=True)
            if not diag:
                before = carry_ref[hd, rows, :]
                carry_scales[i] = jnp.exp(before)
                carry = carry + before
            carry_ref[hd, rows, :] = carry
            slowest[g] = carry if g not in slowest else jnp.maximum(slowest[g], carry)
        for i, (g, h, r) in live:
            rows = slice(r * SB_R, (r + 1) * SB_R)
            hd = g * HEADS_PER_BLOCK + h
            w = jnp.exp(s_ref[i, :, :nkeys[i]])
            pv = _dot(w.astype(BF16), keys(v_ref, g, nkeys[i]))
            if diag:
                acc_ref[hd, rows, :] = pv
            else:
                acc_ref[hd, rows, :] += carry_scales[i] * pv
        return [(jnp.max(slowest[g]) > SB_DEAD).astype(jnp.int32) if g in slowest
                else jnp.int32(0) for g in range(SB_GROUPS)]

    every = tuple(range(SB_GROUPS))
    alive = tile(qi, True, every)

    def cond(state):
        t, alive = state[0], state[1:]
        return jnp.logical_and(t >= 0, sum(alive) > 0)

    def body(state):
        t, alive = state[0], state[1:]

        def subset(groups):
            return lambda: tuple(tile(t, False, groups))

        branches = subset(every)
        for g in range(SB_GROUPS):
            others_dead = sum(a for j, a in enumerate(alive) if j != g) == 0
            branches = functools.partial(lax.cond, others_dead, subset((g,)), branches)
        return (t - 1,) + tuple(branches())

    lax.while_loop(cond, body, (qi - 1,) + tuple(alive))
    for g in range(SB_GROUPS):
        o_ref[0, :, g * LANES:(g + 1) * LANES] = jnp.where(
            lane < HEAD_DIM, acc_ref[g * HEADS_PER_BLOCK], acc_ref[g * HEADS_PER_BLOCK + 1]
        ).astype(BF16)


def _mixers(proj, bias):
    b, s, _ = proj.shape
    width = SB_GROUPS * LANES
    blocks = W_B // width
    base = 3 * W_A // width
    heads = SB_GROUPS * HEADS_PER_BLOCK
    rows = lambda col: pl.BlockSpec((1, SB_T, width), lambda i, hp, j: (i, j, col + hp))
    full = lambda col: pl.BlockSpec((1, s, width), lambda i, hp, j: (i, 0, col + hp))
    out = pl.BlockSpec((1, SB_T, width), lambda i, hp, j: (i, j, hp))
    return pl.pallas_call(
        _mixers_kernel,
        grid=(b, blocks, s // SB_T),
        in_specs=[
            rows(0), full(blocks), full(2 * blocks),
            pl.BlockSpec((heads, ATT_BIAS_TILES, ATT_TQ, LANES), lambda i, hp, j: (hp, 0, 0, 0)),
            rows(base), full(base + blocks), full(base + 2 * blocks),
        ],
        out_specs=[out, out],
        out_shape=[jax.ShapeDtypeStruct((b, s, W_A), BF16),
                   jax.ShapeDtypeStruct((b, s, W_B), BF16)],
        scratch_shapes=[
            pltpu.VMEM((heads, SB_T, LANES), F32),
            pltpu.VMEM((heads, SB_T, 1), F32),
            pltpu.VMEM((heads * (SB_T // SB_R), SB_R, SB_T), F32),
            pltpu.VMEM((heads * (SB_T // ATT_TQ), ATT_TQ, ATT_WIN), F32),
        ],
        compiler_params=pltpu.CompilerParams(
            dimension_semantics=("arbitrary", "arbitrary", "arbitrary"),
            vmem_limit_bytes=VMEM_LIMIT),
        name="token_mixers",
    )(proj, proj, proj, bias, proj, proj, proj)


def _ffn_kernel(x_ref, oa_ref, ob_ref, mod_ref, ga_ref, gb_ref, wout_ref, wup_ref,
                cw_ref, cb_ref, wdown_ref, fg_ref, o_ref, act_ref, tail_ref, *, final):
    tm = FFN_TM

    @pl.when(pl.program_id(1) == 0)
    def _():
        tail_ref[...] = jnp.zeros_like(tail_ref)

    gate_mix = mod_ref[0, 2:3, :]
    shift = mod_ref[0, 3:4, :]
    scale = mod_ref[0, 4:5, :]
    gate_ffn = mod_ref[0, 5:6, :]

    na = _rms(oa_ref[0].astype(F32)) * ga_ref[...]
    nb = _rms(ob_ref[0].astype(F32)) * gb_ref[...]
    mix_in = jnp.concatenate([na, nb], axis=-1).astype(BF16)
    x1 = x_ref[0] + gate_mix * _dot(mix_in, wout_ref[...])

    h = (_rms(x1) * (1.0 + scale) + shift).astype(BF16)
    row = lax.broadcasted_iota(jnp.int32, (tm, FFN_CHUNK), 0)

    def conv(cols):
        up = _dot(h, wup_ref[:, cols])
        prev = tail_ref[:, cols]
        tail_ref[:, cols] = up[tm - 8:, :]
        m1 = jnp.where(row == 0, prev[7:8, :], pltpu.roll(up, 1, axis=0))
        m2 = jnp.where(row == 0, prev[6:7, :],
                       jnp.where(row == 1, prev[7:8, :], pltpu.roll(up, 2, axis=0)))
        return (cw_ref[0:1, cols] * m2 + cw_ref[1:2, cols] * m1
                + cw_ref[2:3, cols] * up + cb_ref[:, cols])

    for j in range(D_FF // FFN_CHUNK):
        g = conv(slice(j * FFN_CHUNK, (j + 1) * FFN_CHUNK))
        v = conv(slice(D_FF + j * FFN_CHUNK, D_FF + (j + 1) * FFN_CHUNK))
        act = g / (1.0 + jnp.exp(-g)) * v
        act_ref[:, j * FFN_CHUNK:(j + 1) * FFN_CHUNK] = act.astype(BF16)

    x2 = x1 + gate_ffn * _dot(act_ref[...], wdown_ref[...])
    if final:
        x2 = _rms(x2) * fg_ref[...]
    o_ref[0] = x2


def _ffn(x, oa, ob, mod, g_a, g_b, w_out, w_up, conv_w, conv_b, w_down, final_g, final):
    b, s, d = x.shape
    tm = FFN_TM
    const = lambda i, j: (0, 0)
    tok = lambda i, j: (i, j, 0)
    return pl.pallas_call(
        functools.partial(_ffn_kernel, final=final),
        grid=(b, s // tm),
        in_specs=[
            pl.BlockSpec((1, tm, d), tok),
            pl.BlockSpec((1, tm, W_A), tok),
            pl.BlockSpec((1, tm, W_B), tok),
            pl.BlockSpec((1, N_MOD, d), lambda i, j: (i, 0, 0)),
            pl.BlockSpec((1, W_A), const),
            pl.BlockSpec((1, W_B), const),
            pl.BlockSpec((W_A + W_B, d), const, pipeline_mode=pl.Buffered(1)),
            pl.BlockSpec((d, 2 * D_FF), const, pipeline_mode=pl.Buffered(1)),
            pl.BlockSpec((3, 2 * D_FF), const),
            pl.BlockSpec((1, 2 * D_FF), const),
            pl.BlockSpec((D_FF, d), const, pipeline_mode=pl.Buffered(1)),
            pl.BlockSpec((1, d), const),
        ],
        out_specs=pl.BlockSpec((1, tm, d), tok),
        out_shape=jax.ShapeDtypeStruct((b, s, d), F32),
        scratch_shapes=[
            pltpu.VMEM((tm, D_FF), BF16),
            pltpu.VMEM((8, 2 * D_FF), F32),
        ],
        compiler_params=pltpu.CompilerParams(
            dimension_semantics=("arbitrary", "arbitrary"), vmem_limit_bytes=VMEM_LIMIT),
        name="out_proj_ffn",
    )(x, oa, ob, mod, g_a, g_b, w_out, w_up, conv_w, conv_b, w_down, final_g)


def kernel(x, c, w_ada, b_ada, w_in, rel_bias, g_a, g_b, w_out, w_up, conv_w, conv_b,
           w_down, final_g):
    depth = w_ada.shape[0]
    b = x.shape[0]
    mod = _modulation(c, w_ada, b_ada).reshape(depth, b, N_MOD, D_MODEL)
    bias = _bias_tables(rel_bias).reshape(depth, H_A, ATT_BIAS_TILES, ATT_TQ, LANES)
    for l in range(depth):
        proj = _in_proj(x, mod[l], w_in[l].astype(BF16))
        oa, ob = _mixers(proj, bias[l])
        x = _ffn(x, oa, ob, mod[l], g_a[l][None], g_b[l][None], w_out[l].astype(BF16),
                 w_up[l].astype(BF16), conv_w[l], conv_b[l][None], w_down[l].astype(BF16),
                 final_g[None], final=(l == depth - 1))
    return x
```

```python
import functools

import jax
import jax.numpy as jnp
from jax import lax
from jax.experimental import pallas as pl
from jax.experimental.pallas import tpu as pltpu

F32 = jnp.float32
BF16 = jnp.bfloat16

D_MODEL = 1024
CHUNK = 64
N_PREV_CHUNKS = 8
HEAD_DIM = 64
H_A = 8
H_B = 8
W_A = H_A * HEAD_DIM
W_B = H_B * HEAD_DIM
W_IN = 3 * (W_A + W_B)
REL_CLIP = 128
D_FF = 2816
N_MOD = 6
EPS = 1e-6
QK_SCALE = HEAD_DIM ** -0.5

LANES = 128
HEADS_PER_BLOCK = LANES // HEAD_DIM
MASKED = -1e30
NEG_LOG2E = -1.4426950408889634

ATT_TQ = 128
ATT_WIN = N_PREV_CHUNKS * CHUNK + ATT_TQ
ATT_BIAS_TILES = (N_PREV_CHUNKS * CHUNK + ATT_WIN) // LANES

SB_T = 256
SB_R = 128
SB_GROUPS = 2
SB_DEAD = -104.0

MOD_TN = 1536
BIAS_UNROLL = 8
CHUNK_SHIFT = CHUNK.bit_length() - 1
IN_PROJ_TM = 1024
FFN_TM = 512
FFN_CHUNK = 256
VMEM_LIMIT = 56 * 1024 * 1024


def _dot(a, b):
    return jnp.dot(a, b, preferred_element_type=F32)


def _dot_nt(a, b):
    return lax.dot_general(a, b, (((1,), (1,)), ((), ())), preferred_element_type=F32)


def _rms(x):
    return x * lax.rsqrt(jnp.mean(x * x, axis=-1, keepdims=True) + EPS)


def _split_bf16(x):
    hi = x.astype(BF16)
    lo = (x - hi.astype(F32)).astype(BF16)
    return hi, lo


def _mod_kernel(c_ref, w_ref, b_ref, o_ref):
    c = c_ref[...]
    ca = c / (1.0 + jnp.exp(-c))
    a_hi, a_lo = _split_bf16(ca)
    w_hi, w_lo = _split_bf16(w_ref[0])
    acc = _dot(a_hi, w_hi) + _dot(a_hi, w_lo) + _dot(a_lo, w_hi)
    o_ref[0] = acc + b_ref[0]


def _modulation(c, w_ada, b_ada):
    depth, d, n = w_ada.shape
    b = c.shape[0]
    tn = MOD_TN
    return pl.pallas_call(
        _mod_kernel,
        grid=(depth, n // tn),
        in_specs=[
            pl.BlockSpec((b, d), lambda l, j: (0, 0)),
            pl.BlockSpec((1, d, tn), lambda l, j: (l, 0, j)),
            pl.BlockSpec((1, 1, tn), lambda l, j: (l, 0, j)),
        ],
        out_specs=pl.BlockSpec((1, b, tn), lambda l, j: (l, 0, j)),
        out_shape=jax.ShapeDtypeStruct((depth, b, n), F32),
        compiler_params=pltpu.CompilerParams(
            dimension_semantics=("arbitrary", "arbitrary"), vmem_limit_bytes=VMEM_LIMIT),
        name="adaln_mod",
    )(c, w_ada, b_ada.reshape(depth, 1, n))


def _bias_kernel(rb_ref, o_ref):
    lh = pl.program_id(0)
    i = lax.broadcasted_iota(jnp.int32, (ATT_TQ, LANES), 0)
    past = N_PREV_CHUNKS * CHUNK
    for ct in range(ATT_BIAS_TILES):
        if (ct + 1) * LANES > ATT_WIN:
            o_ref[0, ct] = jnp.full((ATT_TQ, LANES), MASKED, F32)
            continue
        r = lax.broadcasted_iota(jnp.int32, (ATT_TQ, LANES), 1) + ct * LANES
        dist = jnp.clip(past + i - r, -REL_CLIP, REL_CLIP) + REL_CLIP
        d_lo = min(max(past - (ct + 1) * LANES + 1, -REL_CLIP), REL_CLIP) + REL_CLIP
        d_hi = min(max(past + ATT_TQ - 1 - ct * LANES, -REL_CLIP), REL_CLIP) + REL_CLIP

        def body(d, acc, dist=dist):
            return jnp.where(dist == d, rb_ref[lh, d], acc)

        acc = lax.fori_loop(d_lo, d_hi + 1, body, jnp.zeros((ATT_TQ, LANES), F32),
                            unroll=BIAS_UNROLL)
        qc = i >> CHUNK_SHIFT
        kc = r >> CHUNK_SHIFT
        visible = (kc >= qc) & (kc <= qc + N_PREV_CHUNKS)
        o_ref[0, ct] = jnp.where(visible, acc, MASKED)


def _bias_tables(rel_bias):
    depth, h, nrel = rel_bias.shape
    return pl.pallas_call(
        _bias_kernel,
        grid=(depth * h,),
        in_specs=[pl.BlockSpec(memory_space=pltpu.SMEM)],
        out_specs=pl.BlockSpec((1, ATT_BIAS_TILES, ATT_TQ, LANES), lambda n: (n, 0, 0, 0)),
        out_shape=jax.ShapeDtypeStruct((depth * h, ATT_BIAS_TILES, ATT_TQ, LANES), F32),
        compiler_params=pltpu.CompilerParams(dimension_semantics=("arbitrary",)),
        name="rel_bias_table",
    )(rel_bias.reshape(depth * h, nrel))


def _in_proj_kernel(x_ref, mod_ref, w_ref, o_ref):
    x = x_ref[0]
    shift = mod_ref[0, 0:1, :]
    scale = mod_ref[0, 1:2, :]
    h = _rms(x) * (1.0 + scale) + shift
    o_ref[0] = _dot(h.astype(BF16), w_ref[...]).astype(BF16)


def _in_proj(x, mod, w_in):
    b, s, d = x.shape
    n = w_in.shape[1]
    tm = IN_PROJ_TM
    return pl.pallas_call(
        _in_proj_kernel,
        grid=(b, s // tm),
        in_specs=[
            pl.BlockSpec((1, tm, d), lambda i, j: (i, j, 0)),
            pl.BlockSpec((1, N_MOD, d), lambda i, j: (i, 0, 0)),
            pl.BlockSpec((d, n), lambda i, j: (0, 0)),
        ],
        out_specs=pl.BlockSpec((1, tm, n), lambda i, j: (i, j, 0)),
        out_shape=jax.ShapeDtypeStruct((b, s, n), BF16),
        compiler_params=pltpu.CompilerParams(
            dimension_semantics=("arbitrary", "arbitrary"), vmem_limit_bytes=VMEM_LIMIT),
        name="in_proj",
    )(x, mod, w_in)


def _chunked_block(q_ref, k_ref, v_ref, bias_ref, o_ref, s_ref, q0_block, row_chunks, groups):
    past = N_PREV_CHUNKS * CHUNK
    lane = lax.broadcasted_iota(jnp.int32, (ATT_TQ, LANES), 1)
    chains = [(c, g, h) for c in range(row_chunks) for g in range(groups)
              for h in range(HEADS_PER_BLOCK)]
    start, bias_tile = [], []
    for c in range(row_chunks):
        q0 = q0_block + c * ATT_TQ
        start.append(pl.multiple_of(jnp.maximum(q0 - past, 0), LANES))
        bias_tile.append(jnp.maximum(past - q0, 0) // LANES)

    def window(ref, c, g):
        return ref[0, pl.ds(start[c], ATT_WIN), g * LANES:(g + 1) * LANES]

    for i, (c, g, h) in enumerate(chains):
        qg = q_ref[0, c * ATT_TQ:(c + 1) * ATT_TQ, g * LANES:(g + 1) * LANES] * QK_SCALE
        in_head = (lane >= h * HEAD_DIM) & (lane < (h + 1) * HEAD_DIM)
        bias = jnp.concatenate(
            [bias_ref[g * HEADS_PER_BLOCK + h, bias_tile[c] + t]
             for t in range(ATT_WIN // LANES)], axis=1)
        s_ref[i] = _dot_nt(jnp.where(in_head, qg, jnp.zeros_like(qg)), window(k_ref, c, g)) + bias
    outs = {}
    for i, (c, g, h) in enumerate(chains):
        m = jnp.max(s_ref[i], axis=-1, keepdims=True)
        p = jnp.exp(s_ref[i] - m)
        denom = jnp.sum(p, axis=-1, keepdims=True)
        outs[c, g, h] = _dot(p.astype(BF16), window(v_ref, c, g)) / denom
    for c in range(row_chunks):
        for g in range(groups):
            o_ref[0, c * ATT_TQ:(c + 1) * ATT_TQ, g * LANES:(g + 1) * LANES] = jnp.where(
                lane < HEAD_DIM, outs[c, g, 0], outs[c, g, 1]).astype(BF16)


def _mixers_kernel(qa_ref, ka_ref, va_ref, bias_ref, q_ref, k_ref, v_ref, oa_ref, o_ref,
                   acc_ref, carry_ref, s_ref, sa_ref):
    qi = pl.program_id(2)
    _chunked_block(qa_ref, ka_ref, va_ref, bias_ref, oa_ref, sa_ref,
                   qi * SB_T, SB_T // ATT_TQ, SB_GROUPS)
    lane = lax.broadcasted_iota(jnp.int32, (SB_T, LANES), 1)

    def below_diagonal(rows, cols, row0):
        r = lax.broadcasted_iota(jnp.int32, (rows, cols), 0) + row0
        return r > lax.broadcasted_iota(jnp.int32, (rows, cols), 1)

    later = {n: jnp.where(below_diagonal(n, n, 0), 1.0, 0.0).astype(BF16)
             for n in range(LANES, SB_T + 1, LANES)}

    chains = [(g, h, r) for g in range(SB_GROUPS) for h in range(HEADS_PER_BLOCK)
              for r in range(SB_T // SB_R)]
    qm = {}
    for g in range(SB_GROUPS):
        qg = q_ref[0, :, g * LANES:(g + 1) * LANES] * QK_SCALE
        for h in range(HEADS_PER_BLOCK):
            in_head = (lane >= h * HEAD_DIM) & (lane < (h + 1) * HEAD_DIM)
            qm[g, h] = jnp.where(in_head, qg, jnp.zeros_like(qg))

    def tile(t, diag, groups, slot0=0, valid=None):
        ks = pl.multiple_of(t * SB_T, SB_T)
        nkeys = [-(-SB_R * (r + 1) // LANES) * LANES if diag else SB_T for _, _, r in chains]
        live = [(i, c) for i, c in enumerate(chains) if c[0] in groups]

        def keys(ref, g, n):
            return ref[0, pl.ds(ks, n), g * LANES:(g + 1) * LANES]

        for i, (g, h, r) in live:
            s_ref[slot0 + i, :, :nkeys[i]] = _dot_nt(qm[g, h][r * SB_R:(r + 1) * SB_R],
                                                     keys(k_ref, g, nkeys[i]))
        carry_scales = {}
        slowest = {}
        for i, (g, h, r) in live:
            rows = slice(r * SB_R, (r + 1) * SB_R)
            hd = g * HEADS_PER_BLOCK + h
            nk = nkeys[i]
            s = s_ref[slot0 + i, :, :nk]
            if diag:
                s = jnp.where(below_diagonal(SB_R, nk, r * SB_R), s, MASKED)
            log_beta = jnp.minimum(s, 0.0) - jnp.log(1.0 + jnp.exp2(jnp.abs(s) * NEG_LOG2E))
            log_keep = log_beta - s
            s_ref[slot0 + i, :, :nk] = log_beta + _dot(log_keep.astype(BF16), later[nk])
            carry = jnp.sum(log_keep, axis=-1, keepdims=True)
            if not diag:
                before = carry_ref[hd, rows, :]
                carry_scales[i] = jnp.exp(before)
                if valid is not None:
                    carry_scales[i] = carry_scales[i] * valid
                    carry = carry * valid
                carry = carry + before
            carry_ref[hd, rows, :] = carry
            slowest[g] = carry if g not in slowest else jnp.maximum(slowest[g], carry)
        for i, (g, h, r) in live:
            rows = slice(r * SB_R, (r + 1) * SB_R)
            hd = g * HEADS_PER_BLOCK + h
            w = jnp.exp(s_ref[slot0 + i, :, :nkeys[i]])
            pv = _dot(w.astype(BF16), keys(v_ref, g, nkeys[i]))
            if diag:
                acc_ref[hd, rows, :] = pv
            else:
                acc_ref[hd, rows, :] += carry_scales[i] * pv
        return [(jnp.max(slowest[g]) > SB_DEAD).astype(jnp.int32) if g in slowest
                else jnp.int32(0) for g in range(SB_GROUPS)]

    every = tuple(range(SB_GROUPS))
    tile(qi, True, every)
    alive = tile(jnp.maximum(qi - 1, 0), False, every, slot0=len(chains),
                 valid=(qi > 0).astype(F32))

    def cond(state):
        t, alive = state[0], state[1:]
        return jnp.logical_and(t >= 0, sum(alive) > 0)

    def body(state):
        t, alive = state[0], state[1:]

        def subset(groups):
            return lambda: tuple(tile(t, False, groups))

        branches = subset(every)
        for g in range(SB_GROUPS):
            others_dead = sum(a for j, a in enumerate(alive) if j != g) == 0
            branches = functools.partial(lax.cond, others_dead, subset((g,)), branches)
        return (t - 1,) + tuple(branches())

    lax.while_loop(cond, body, (qi - 2,) + tuple(alive))
    for g in range(SB_GROUPS):
        o_ref[0, :, g * LANES:(g + 1) * LANES] = jnp.where(
            lane < HEAD_DIM, acc_ref[g * HEADS_PER_BLOCK], acc_ref[g * HEADS_PER_BLOCK + 1]
        ).astype(BF16)


def _mixers(proj, bias):
    b, s, _ = proj.shape
    width = SB_GROUPS * LANES
    blocks = W_B // width
    base = 3 * W_A // width
    heads = SB_GROUPS * HEADS_PER_BLOCK
    rows = lambda col: pl.BlockSpec((1, SB_T, width), lambda i, hp, j: (i, j, col + hp))
    full = lambda col: pl.BlockSpec((1, s, width), lambda i, hp, j: (i, 0, col + hp))
    out = pl.BlockSpec((1, SB_T, width), lambda i, hp, j: (i, j, hp))
    return pl.pallas_call(
        _mixers_kernel,
        grid=(b, blocks, s // SB_T),
        in_specs=[
            rows(0), full(blocks), full(2 * blocks),
            pl.BlockSpec((heads, ATT_BIAS_TILES, ATT_TQ, LANES), lambda i, hp, j: (hp, 0, 0, 0)),
            rows(base), full(base + blocks), full(base + 2 * blocks),
        ],
        out_specs=[out, out],
        out_shape=[jax.ShapeDtypeStruct((b, s, W_A), BF16),
                   jax.ShapeDtypeStruct((b, s, W_B), BF16)],
        scratch_shapes=[
            pltpu.VMEM((heads, SB_T, LANES), F32),
            pltpu.VMEM((heads, SB_T, 1), F32),
            pltpu.VMEM((2 * heads * (SB_T // SB_R), SB_R, SB_T), F32),
            pltpu.VMEM((heads * (SB_T // ATT_TQ), ATT_TQ, ATT_WIN), F32),
        ],
        compiler_params=pltpu.CompilerParams(
            dimension_semantics=("arbitrary", "arbitrary", "arbitrary"),
            vmem_limit_bytes=VMEM_LIMIT),
        name="token_mixers",
    )(proj, proj, proj, bias, proj, proj, proj)


def _ffn_kernel(x_ref, oa_ref, ob_ref, mod_ref, ga_ref, gb_ref, wout_ref, wup_ref,
                cw_ref, cb_ref, wdown_ref, fg_ref, o_ref, act_ref, tail_ref, *, final):
    tm = FFN_TM

    @pl.when(pl.program_id(1) == 0)
    def _():
        tail_ref[...] = jnp.zeros_like(tail_ref)

    gate_mix = mod_ref[0, 2:3, :]
    shift = mod_ref[0, 3:4, :]
    scale = mod_ref[0, 4:5, :]
    gate_ffn = mod_ref[0, 5:6, :]

    na = _rms(oa_ref[0].astype(F32)) * ga_ref[...]
    nb = _rms(ob_ref[0].astype(F32)) * gb_ref[...]
    mix_in = jnp.concatenate([na, nb], axis=-1).astype(BF16)
    x1 = x_ref[0] + gate_mix * _dot(mix_in, wout_ref[...])

    h = (_rms(x1) * (1.0 + scale) + shift).astype(BF16)
    row = lax.broadcasted_iota(jnp.int32, (tm, FFN_CHUNK), 0)

    def conv(cols):
        up = _dot(h, wup_ref[:, cols])
        prev = tail_ref[:, cols]
        tail_ref[:, cols] = up[tm - 8:, :]
        m1 = jnp.where(row == 0, prev[7:8, :], pltpu.roll(up, 1, axis=0))
        m2 = jnp.where(row == 0, prev[6:7, :],
                       jnp.where(row == 1, prev[7:8, :], pltpu.roll(up, 2, axis=0)))
        return (cw_ref[0:1, cols] * m2 + cw_ref[1:2, cols] * m1
                + cw_ref[2:3, cols] * up + cb_ref[:, cols])

    for j in range(D_FF // FFN_CHUNK):
        g = conv(slice(j * FFN_CHUNK, (j + 1) * FFN_CHUNK))
        v = conv(slice(D_FF + j * FFN_CHUNK, D_FF + (j + 1) * FFN_CHUNK))
        act = g / (1.0 + jnp.exp(-g)) * v
        act_ref[:, j * FFN_CHUNK:(j + 1) * FFN_CHUNK] = act.astype(BF16)

    x2 = x1 + gate_ffn * _dot(act_ref[...], wdown_ref[...])
    if final:
        x2 = _rms(x2) * fg_ref[...]
    o_ref[0] = x2


def _ffn(x, oa, ob, mod, g_a, g_b, w_out, w_up, conv_w, conv_b, w_down, final_g, final):
    b, s, d = x.shape
    tm = FFN_TM
    const = lambda i, j: (0, 0)
    tok = lambda i, j: (i, j, 0)
    return pl.pallas_call(
        functools.partial(_ffn_kernel, final=final),
        grid=(b, s // tm),
        in_specs=[
            pl.BlockSpec((1, tm, d), tok),
            pl.BlockSpec((1, tm, W_A), tok),
            pl.BlockSpec((1, tm, W_B), tok),
            pl.BlockSpec((1, N_MOD, d), lambda i, j: (i, 0, 0)),
            pl.BlockSpec((1, W_A), const),
            pl.BlockSpec((1, W_B), const),
            pl.BlockSpec((W_A + W_B, d), const, pipeline_mode=pl.Buffered(1)),
            pl.BlockSpec((d, 2 * D_FF), const, pipeline_mode=pl.Buffered(1)),
            pl.BlockSpec((3, 2 * D_FF), const),
            pl.BlockSpec((1, 2 * D_FF), const),
            pl.BlockSpec((D_FF, d), const, pipeline_mode=pl.Buffered(1)),
            pl.BlockSpec((1, d), const),
        ],
        out_specs=pl.BlockSpec((1, tm, d), tok),
        out_shape=jax.ShapeDtypeStruct((b, s, d), F32),
        scratch_shapes=[
            pltpu.VMEM((tm, D_FF), BF16),
            pltpu.VMEM((8, 2 * D_FF), F32),
        ],
        compiler_params=pltpu.CompilerParams(
            dimension_semantics=("arbitrary", "arbitrary"), vmem_limit_bytes=VMEM_LIMIT),
        name="out_proj_ffn",
    )(x, oa, ob, mod, g_a, g_b, w_out, w_up, conv_w, conv_b, w_down, final_g)


def kernel(x, c, w_ada, b_ada, w_in, rel_bias, g_a, g_b, w_out, w_up, conv_w, conv_b,
           w_down, final_g):
    depth = w_ada.shape[0]
    b = x.shape[0]
    mod = _modulation(c, w_ada, b_ada).reshape(depth, b, N_MOD, D_MODEL)
    bias = _bias_tables(rel_bias).reshape(depth, H_A, ATT_BIAS_TILES, ATT_TQ, LANES)
    for l in range(depth):
        proj = _in_proj(x, mod[l], w_in[l].astype(BF16))
        oa, ob = _mixers(proj, bias[l])
        x = _ffn(x, oa, ob, mod[l], g_a[l][None], g_b[l][None], w_out[l].astype(BF16),
                 w_up[l].astype(BF16), conv_w[l], conv_b[l][None], w_down[l].astype(BF16),
                 final_g[None], final=(l == depth - 1))
    return x
```

```python
import functools

import jax
import jax.numpy as jnp
from jax import lax
from jax.experimental import pallas as pl
from jax.experimental.pallas import tpu as pltpu

F32 = jnp.float32
BF16 = jnp.bfloat16

D_MODEL = 1024
CHUNK = 64
N_PREV_CHUNKS = 8
HEAD_DIM = 64
H_A = 8
H_B = 8
W_A = H_A * HEAD_DIM
W_B = H_B * HEAD_DIM
W_IN = 3 * (W_A + W_B)
REL_CLIP = 128
D_FF = 2816
N_MOD = 6
EPS = 1e-6
QK_SCALE = HEAD_DIM ** -0.5

LANES = 128
HEADS_PER_BLOCK = LANES // HEAD_DIM
MASKED = -1e30
NEG_LOG2E = -1.4426950408889634

ATT_TQ = 128
ATT_WIN = N_PREV_CHUNKS * CHUNK + ATT_TQ
ATT_BIAS_TILES = (N_PREV_CHUNKS * CHUNK + ATT_WIN) // LANES

SB_T = 256
SB_R = 128
SB_GROUPS = 2
SB_DEAD = -104.0

MOD_TN = 1536
BIAS_UNROLL = 8
CHUNK_SHIFT = CHUNK.bit_length() - 1
IN_PROJ_TM = 1024
FFN_TM = 512
FFN_CHUNK = 256
VMEM_LIMIT = 56 * 1024 * 1024


def _dot(a, b):
    return jnp.dot(a, b, preferred_element_type=F32)


def _dot_nt(a, b):
    return lax.dot_general(a, b, (((1,), (1,)), ((), ())), preferred_element_type=F32)


def _rms(x):
    return x * lax.rsqrt(jnp.mean(x * x, axis=-1, keepdims=True) + EPS)


def _split_bf16(x):
    hi = x.astype(BF16)
    lo = (x - hi.astype(F32)).astype(BF16)
    return hi, lo


def _mod_kernel(c_ref, w_ref, b_ref, o_ref):
    c = c_ref[...]
    ca = c / (1.0 + jnp.exp(-c))
    a_hi, a_lo = _split_bf16(ca)
    w_hi, w_lo = _split_bf16(w_ref[0])
    acc = _dot(a_hi, w_hi) + _dot(a_hi, w_lo) + _dot(a_lo, w_hi)
    o_ref[0] = acc + b_ref[0]


def _modulation(c, w_ada, b_ada):
    depth, d, n = w_ada.shape
    b = c.shape[0]
    tn = MOD_TN
    return pl.pallas_call(
        _mod_kernel,
        grid=(depth, n // tn),
        in_specs=[
            pl.BlockSpec((b, d), lambda l, j: (0, 0)),
            pl.BlockSpec((1, d, tn), lambda l, j: (l, 0, j)),
            pl.BlockSpec((1, 1, tn), lambda l, j: (l, 0, j)),
        ],
        out_specs=pl.BlockSpec((1, b, tn), lambda l, j: (l, 0, j)),
        out_shape=jax.ShapeDtypeStruct((depth, b, n), F32),
        compiler_params=pltpu.CompilerParams(
            dimension_semantics=("arbitrary", "arbitrary"), vmem_limit_bytes=VMEM_LIMIT),
        name="adaln_mod",
    )(c, w_ada, b_ada.reshape(depth, 1, n))


def _bias_kernel(rb_ref, o_ref):
    lh = pl.program_id(0)
    i = lax.broadcasted_iota(jnp.int32, (ATT_TQ, LANES), 0)
    past = N_PREV_CHUNKS * CHUNK
    for ct in range(ATT_BIAS_TILES):
        if (ct + 1) * LANES > ATT_WIN:
            o_ref[0, ct] = jnp.full((ATT_TQ, LANES), MASKED, F32)
            continue
        r = lax.broadcasted_iota(jnp.int32, (ATT_TQ, LANES), 1) + ct * LANES
        dist = jnp.clip(past + i - r, -REL_CLIP, REL_CLIP) + REL_CLIP
        d_lo = min(max(past - (ct + 1) * LANES + 1, -REL_CLIP), REL_CLIP) + REL_CLIP
        d_hi = min(max(past + ATT_TQ - 1 - ct * LANES, -REL_CLIP), REL_CLIP) + REL_CLIP

        def body(d, acc, dist=dist):
            return jnp.where(dist == d, rb_ref[lh, d], acc)

        acc = lax.fori_loop(d_lo, d_hi + 1, body, jnp.zeros((ATT_TQ, LANES), F32),
                            unroll=BIAS_UNROLL)
        qc = i >> CHUNK_SHIFT
        kc = r >> CHUNK_SHIFT
        visible = (kc >= qc) & (kc <= qc + N_PREV_CHUNKS)
        o_ref[0, ct] = jnp.where(visible, acc, MASKED)


def _bias_tables(rel_bias):
    depth, h, nrel = rel_bias.shape
    return pl.pallas_call(
        _bias_kernel,
        grid=(depth * h,),
        in_specs=[pl.BlockSpec(memory_space=pltpu.SMEM)],
        out_specs=pl.BlockSpec((1, ATT_BIAS_TILES, ATT_TQ, LANES), lambda n: (n, 0, 0, 0)),
        out_shape=jax.ShapeDtypeStruct((depth * h, ATT_BIAS_TILES, ATT_TQ, LANES), F32),
        compiler_params=pltpu.CompilerParams(dimension_semantics=("arbitrary",)),
        name="rel_bias_table",
    )(rel_bias.reshape(depth * h, nrel))


def _in_proj_kernel(x_ref, mod_ref, w_ref, o_ref):
    x = x_ref[0]
    shift = mod_ref[0, 0:1, :]
    scale = mod_ref[0, 1:2, :]
    h = _rms(x) * (1.0 + scale) + shift
    o_ref[0] = _dot(h.astype(BF16), w_ref[0]).astype(BF16)


def _in_proj(x, mod, w_in, layer):
    b, s, d = x.shape
    n = w_in.shape[2]
    tm = IN_PROJ_TM
    return pl.pallas_call(
        _in_proj_kernel,
        grid=(b, s // tm),
        in_specs=[
            pl.BlockSpec((1, tm, d), lambda i, j: (i, j, 0)),
            pl.BlockSpec((1, N_MOD, d), lambda i, j: (i, 0, 0)),
            pl.BlockSpec((1, d, n), lambda i, j: (layer, 0, 0)),
        ],
        out_specs=pl.BlockSpec((1, tm, n), lambda i, j: (i, j, 0)),
        out_shape=jax.ShapeDtypeStruct((b, s, n), BF16),
        compiler_params=pltpu.CompilerParams(
            dimension_semantics=("arbitrary", "arbitrary"), vmem_limit_bytes=VMEM_LIMIT),
        name="in_proj",
    )(x, mod, w_in)


def _chunked_block(q_ref, k_ref, v_ref, bias_ref, o_ref, s_ref, q0_block, row_chunks, groups):
    past = N_PREV_CHUNKS * CHUNK
    lane = lax.broadcasted_iota(jnp.int32, (ATT_TQ, LANES), 1)
    chains = [(c, g, h) for c in range(row_chunks) for g in range(groups)
              for h in range(HEADS_PER_BLOCK)]
    start, bias_tile = [], []
    for c in range(row_chunks):
        q0 = q0_block + c * ATT_TQ
        start.append(pl.multiple_of(jnp.maximum(q0 - past, 0), LANES))
        bias_tile.append(jnp.maximum(past - q0, 0) // LANES)

    def window(ref, c, g):
        return ref[0, pl.ds(start[c], ATT_WIN), g * LANES:(g + 1) * LANES]

    for i, (c, g, h) in enumerate(chains):
        qg = q_ref[0, c * ATT_TQ:(c + 1) * ATT_TQ, g * LANES:(g + 1) * LANES] * QK_SCALE
        in_head = (lane >= h * HEAD_DIM) & (lane < (h + 1) * HEAD_DIM)
        bias = jnp.concatenate(
            [bias_ref[g * HEADS_PER_BLOCK + h, bias_tile[c] + t]
             for t in range(ATT_WIN // LANES)], axis=1)
        s_ref[i] = _dot_nt(jnp.where(in_head, qg, jnp.zeros_like(qg)), window(k_ref, c, g)) + bias
    outs = {}
    for i, (c, g, h) in enumerate(chains):
        m = jnp.max(s_ref[i], axis=-1, keepdims=True)
        p = jnp.exp(s_ref[i] - m)
        denom = jnp.sum(p, axis=-1, keepdims=True)
        outs[c, g, h] = _dot(p.astype(BF16), window(v_ref, c, g)) / denom
    for c in range(row_chunks):
        for g in range(groups):
            o_ref[0, c * ATT_TQ:(c + 1) * ATT_TQ, g * LANES:(g + 1) * LANES] = jnp.where(
                lane < HEAD_DIM, outs[c, g, 0], outs[c, g, 1]).astype(BF16)


def _mixers_kernel(qa_ref, ka_ref, va_ref, bias_ref, q_ref, k_ref, v_ref, oa_ref, o_ref,
                   acc_ref, carry_ref, s_ref, sa_ref):
    qi = pl.program_id(2)
    _chunked_block(qa_ref, ka_ref, va_ref, bias_ref, oa_ref, sa_ref,
                   qi * SB_T, SB_T // ATT_TQ, SB_GROUPS)
    lane = lax.broadcasted_iota(jnp.int32, (SB_T, LANES), 1)

    def below_diagonal(rows, cols, row0):
        r = lax.broadcasted_iota(jnp.int32, (rows, cols), 0) + row0
        return r > lax.broadcasted_iota(jnp.int32, (rows, cols), 1)

    later = {n: jnp.where(below_diagonal(n, n, 0), 1.0, 0.0).astype(BF16)
             for n in range(LANES, SB_T + 1, LANES)}

    chains = [(g, h, r) for g in range(SB_GROUPS) for h in range(HEADS_PER_BLOCK)
              for r in range(SB_T // SB_R)]
    qm = {}
    for g in range(SB_GROUPS):
        qg = q_ref[0, :, g * LANES:(g + 1) * LANES] * QK_SCALE
        for h in range(HEADS_PER_BLOCK):
            in_head = (lane >= h * HEAD_DIM) & (lane < (h + 1) * HEAD_DIM)
            qm[g, h] = jnp.where(in_head, qg, jnp.zeros_like(qg))

    def tile(t, diag, groups, slot0=0, valid=None):
        ks = pl.multiple_of(t * SB_T, SB_T)
        nkeys = [-(-SB_R * (r + 1) // LANES) * LANES if diag else SB_T for _, _, r in chains]
        live = [(i, c) for i, c in enumerate(chains) if c[0] in groups]

        def keys(ref, g, n):
            return ref[0, pl.ds(ks, n), g * LANES:(g + 1) * LANES]

        for i, (g, h, r) in live:
            s_ref[slot0 + i, :, :nkeys[i]] = _dot_nt(qm[g, h][r * SB_R:(r + 1) * SB_R],
                                                     keys(k_ref, g, nkeys[i]))
        carry_scales = {}
        slowest = {}
        for i, (g, h, r) in live:
            rows = slice(r * SB_R, (r + 1) * SB_R)
            hd = g * HEADS_PER_BLOCK + h
            nk = nkeys[i]
            s = s_ref[slot0 + i, :, :nk]
            if diag:
                s = jnp.where(below_diagonal(SB_R, nk, r * SB_R), s, MASKED)
            log_beta = jnp.minimum(s, 0.0) - jnp.log(1.0 + jnp.exp2(jnp.abs(s) * NEG_LOG2E))
            log_keep = log_beta - s
            s_ref[slot0 + i, :, :nk] = log_beta + _dot(log_keep.astype(BF16), later[nk])
            carry = jnp.sum(log_keep, axis=-1, keepdims=True)
            if not diag:
                before = carry_ref[hd, rows, :]
                carry_scales[i] = jnp.exp(before)
                if valid is not None:
                    carry_scales[i] = carry_scales[i] * valid
                    carry = carry * valid
                carry = carry + before
            carry_ref[hd, rows, :] = carry
            slowest[g] = carry if g not in slowest else jnp.maximum(slowest[g], carry)
        for i, (g, h, r) in live:
            rows = slice(r * SB_R, (r + 1) * SB_R)
            hd = g * HEADS_PER_BLOCK + h
            w = jnp.exp(s_ref[slot0 + i, :, :nkeys[i]])
            pv = _dot(w.astype(BF16), keys(v_ref, g, nkeys[i]))
            if diag:
                acc_ref[hd, rows, :] = pv
            else:
                acc_ref[hd, rows, :] += carry_scales[i] * pv
        return [(jnp.max(slowest[g]) > SB_DEAD).astype(jnp.int32) if g in slowest
                else jnp.int32(0) for g in range(SB_GROUPS)]

    every = tuple(range(SB_GROUPS))
    tile(qi, True, every)
    alive = tile(jnp.maximum(qi - 1, 0), False, every, slot0=len(chains),
                 valid=(qi > 0).astype(F32))

    def cond(state):
        t, alive = state[0], state[1:]
        return jnp.logical_and(t >= 0, sum(alive) > 0)

    def body(state):
        t, alive = state[0], state[1:]

        def subset(groups):
            return lambda: tuple(tile(t, False, groups))

        branches = subset(every)
        for g in range(SB_GROUPS):
            others_dead = sum(a for j, a in enumerate(alive) if j != g) == 0
            branches = functools.partial(lax.cond, others_dead, subset((g,)), branches)
        return (t - 1,) + tuple(branches())

    lax.while_loop(cond, body, (qi - 2,) + tuple(alive))
    for g in range(SB_GROUPS):
        o_ref[0, :, g * LANES:(g + 1) * LANES] = jnp.where(
            lane < HEAD_DIM, acc_ref[g * HEADS_PER_BLOCK], acc_ref[g * HEADS_PER_BLOCK + 1]
        ).astype(BF16)


def _mixers(proj, bias):
    b, s, _ = proj.shape
    width = SB_GROUPS * LANES
    blocks = W_B // width
    base = 3 * W_A // width
    heads = SB_GROUPS * HEADS_PER_BLOCK
    rows = lambda col: pl.BlockSpec((1, SB_T, width), lambda i, hp, j: (i, j, col + hp))
    full = lambda col: pl.BlockSpec((1, s, width), lambda i, hp, j: (i, 0, col + hp))
    out = pl.BlockSpec((1, SB_T, width), lambda i, hp, j: (i, j, hp))
    return pl.pallas_call(
        _mixers_kernel,
        grid=(b, blocks, s // SB_T),
        in_specs=[
            rows(0), full(blocks), full(2 * blocks),
            pl.BlockSpec((heads, ATT_BIAS_TILES, ATT_TQ, LANES), lambda i, hp, j: (hp, 0, 0, 0)),
            rows(base), full(base + blocks), full(base + 2 * blocks),
        ],
        out_specs=[out, out],
        out_shape=[jax.ShapeDtypeStruct((b, s, W_A), BF16),
                   jax.ShapeDtypeStruct((b, s, W_B), BF16)],
        scratch_shapes=[
            pltpu.VMEM((heads, SB_T, LANES), F32),
            pltpu.VMEM((heads, SB_T, 1), F32),
            pltpu.VMEM((2 * heads * (SB_T // SB_R), SB_R, SB_T), F32),
            pltpu.VMEM((heads * (SB_T // ATT_TQ), ATT_TQ, ATT_WIN), F32),
        ],
        compiler_params=pltpu.CompilerParams(
            dimension_semantics=("arbitrary", "arbitrary", "arbitrary"),
            vmem_limit_bytes=VMEM_LIMIT),
        name="token_mixers",
    )(proj, proj, proj, bias, proj, proj, proj)


def _ffn_kernel(x_ref, oa_ref, ob_ref, mod_ref, ga_ref, gb_ref, wout_ref, wup_ref,
                cw_ref, cb_ref, wdown_ref, fg_ref, o_ref, act_ref, tail_ref, *, final):
    tm = FFN_TM

    @pl.when(pl.program_id(1) == 0)
    def _():
        tail_ref[...] = jnp.zeros_like(tail_ref)

    gate_mix = mod_ref[0, 2:3, :]
    shift = mod_ref[0, 3:4, :]
    scale = mod_ref[0, 4:5, :]
    gate_ffn = mod_ref[0, 5:6, :]

    na = _rms(oa_ref[0].astype(F32)) * ga_ref[...]
    nb = _rms(ob_ref[0].astype(F32)) * gb_ref[...]
    mix_in = jnp.concatenate([na, nb], axis=-1).astype(BF16)
    x1 = x_ref[0] + gate_mix * _dot(mix_in, wout_ref[0])

    h = (_rms(x1) * (1.0 + scale) + shift).astype(BF16)
    row = lax.broadcasted_iota(jnp.int32, (tm, FFN_CHUNK), 0)

    def conv(cols):
        up = _dot(h, wup_ref[0, :, cols])
        prev = tail_ref[:, cols]
        tail_ref[:, cols] = up[tm - 8:, :]
        m1 = jnp.where(row == 0, prev[7:8, :], pltpu.roll(up, 1, axis=0))
        m2 = jnp.where(row == 0, prev[6:7, :],
                       jnp.where(row == 1, prev[7:8, :], pltpu.roll(up, 2, axis=0)))
        return (cw_ref[0:1, cols] * m2 + cw_ref[1:2, cols] * m1
                + cw_ref[2:3, cols] * up + cb_ref[:, cols])

    for j in range(D_FF // FFN_CHUNK):
        g = conv(slice(j * FFN_CHUNK, (j + 1) * FFN_CHUNK))
        v = conv(slice(D_FF + j * FFN_CHUNK, D_FF + (j + 1) * FFN_CHUNK))
        act = g / (1.0 + jnp.exp(-g)) * v
        act_ref[:, j * FFN_CHUNK:(j + 1) * FFN_CHUNK] = act.astype(BF16)

    x2 = x1 + gate_ffn * _dot(act_ref[...], wdown_ref[0])
    if final:
        x2 = _rms(x2) * fg_ref[...]
    o_ref[0] = x2


def _ffn(x, oa, ob, mod, g_a, g_b, w_out, w_up, conv_w, conv_b, w_down, final_g, layer,
         final):
    b, s, d = x.shape
    weights = lambda i, j: (layer, 0, 0)
    tm = FFN_TM
    const = lambda i, j: (0, 0)
    tok = lambda i, j: (i, j, 0)
    return pl.pallas_call(
        functools.partial(_ffn_kernel, final=final),
        grid=(b, s // tm),
        in_specs=[
            pl.BlockSpec((1, tm, d), tok),
            pl.BlockSpec((1, tm, W_A), tok),
            pl.BlockSpec((1, tm, W_B), tok),
            pl.BlockSpec((1, N_MOD, d), lambda i, j: (i, 0, 0)),
            pl.BlockSpec((1, W_A), const),
            pl.BlockSpec((1, W_B), const),
            pl.BlockSpec((1, W_A + W_B, d), weights, pipeline_mode=pl.Buffered(1)),
            pl.BlockSpec((1, d, 2 * D_FF), weights, pipeline_mode=pl.Buffered(1)),
            pl.BlockSpec((3, 2 * D_FF), const),
            pl.BlockSpec((1, 2 * D_FF), const),
            pl.BlockSpec((1, D_FF, d), weights, pipeline_mode=pl.Buffered(1)),
            pl.BlockSpec((1, d), const),
        ],
        out_specs=pl.BlockSpec((1, tm, d), tok),
        out_shape=jax.ShapeDtypeStruct((b, s, d), F32),
        scratch_shapes=[
            pltpu.VMEM((tm, D_FF), BF16),
            pltpu.VMEM((8, 2 * D_FF), F32),
        ],
        compiler_params=pltpu.CompilerParams(
            dimension_semantics=("arbitrary", "arbitrary"), vmem_limit_bytes=VMEM_LIMIT),
        name="out_proj_ffn",
    )(x, oa, ob, mod, g_a, g_b, w_out, w_up, conv_w, conv_b, w_down, final_g)


def kernel(x, c, w_ada, b_ada, w_in, rel_bias, g_a, g_b, w_out, w_up, conv_w, conv_b,
           w_down, final_g):
    depth = w_ada.shape[0]
    b = x.shape[0]
    mod = _modulation(c, w_ada, b_ada).reshape(depth, b, N_MOD, D_MODEL)
    bias = _bias_tables(rel_bias).reshape(depth, H_A, ATT_BIAS_TILES, ATT_TQ, LANES)
    w_in, w_out, w_up, w_down = (w.astype(BF16) for w in (w_in, w_out, w_up, w_down))
    for l in range(depth):
        proj = _in_proj(x, mod[l], w_in, l)
        oa, ob = _mixers(proj, bias[l])
        x = _ffn(x, oa, ob, mod[l], g_a[l][None], g_b[l][None], w_out, w_up, conv_w[l],
                 conv_b[l][None], w_down, final_g[None], l, final=(l == depth - 1))
    return x
```
